```python
import math
import jax, jax.numpy as jnp
from jax import lax
import numpy as np

D_MODEL = 1024
BATCH = 8
SEQ = 2048
DEPTH = 4
DEC_BATCH = 128
DEC_SEQ = 1
PAST_LEN = 2048
PAGE_SIZE = 128

S5_WIDTH = D_MODEL // 4
S5_GROUP = 16
S5_GROUPS = S5_WIDTH // S5_GROUP
S5_STATE = 64
SB_HEADS = 4
SB_HEAD_DIM = 64
SB_WIDTH = SB_HEADS * SB_HEAD_DIM
SB_BLOCK = 128
SB_BIAS_INIT = -7.0
RET_WIDTH = D_MODEL - S5_WIDTH - SB_WIDTH
RET_HEADS = 4
RET_DV = RET_WIDTH // RET_HEADS
RET_DK = RET_DV // 2
RET_QK = RET_HEADS * RET_DK
RET_CHUNK = 128
ROPE_BASE = 10000.0
D_FF = 2816
CONV_W = 3
LN_EPS = 1e-5
RMS_EPS = 1e-6
ALPHA = (2.0 * DEPTH) ** 0.25
BETA = (8.0 * DEPTH) ** -0.25
SPLITS = [S5_WIDTH, SB_WIDTH, SB_WIDTH, SB_WIDTH, RET_QK, RET_QK, RET_WIDTH, RET_WIDTH]
D_IN = sum(SPLITS)

kernel_name = "hybrid_s5_stickbreak_retention_convffn_step"

F32 = jnp.float32


def layer_norm(x, g, b):
    xf = x.astype(F32)
    mu = jnp.mean(xf, -1, keepdims=True)
    var = jnp.mean(jnp.square(xf - mu), -1, keepdims=True)
    return ((xf - mu) * lax.rsqrt(var + LN_EPS) * g + b).astype(x.dtype)


def rms_norm(x, g):
    xf = x.astype(F32)
    return (xf * lax.rsqrt(jnp.mean(xf * xf, -1, keepdims=True) + RMS_EPS) * g).astype(x.dtype)


def s5_discretize(lam_re, lam_im, log_step, b_re, b_im):
    lam_re = jnp.minimum(lam_re.astype(F32), -1e-4)
    lam_im = lam_im.astype(F32)
    dt = jnp.exp(log_step.astype(F32))[:, None]
    ldt_re, ldt_im = lam_re * dt, lam_im * dt
    mag = jnp.exp(ldt_re)
    a_re, a_im = mag * jnp.cos(ldt_im), mag * jnp.sin(ldt_im)
    den = lam_re * lam_re + lam_im * lam_im
    n_re, n_im = a_re - 1.0, a_im
    f_re = (n_re * lam_re + n_im * lam_im) / den
    f_im = (n_im * lam_re - n_re * lam_im) / den
    b_re, b_im = b_re.astype(F32), b_im.astype(F32)
    bb_re = f_re[..., None] * b_re - f_im[..., None] * b_im
    bb_im = f_re[..., None] * b_im + f_im[..., None] * b_re
    return ldt_re, ldt_im, a_re, a_im, bb_re, bb_im


def s5_combine(e1, e2):
    a1r, a1i, b1r, b1i = e1
    a2r, a2i, b2r, b2i = e2
    return (a2r * a1r - a2i * a1i, a2r * a1i + a2i * a1r,
            a2r * b1r - a2i * b1i + b2r, a2r * b1i + a2i * b1r + b2i)


def s5_branch(u, x0_re, x0_im, lam_re, lam_im, log_step, b_re, b_im, c_re, c_im, d, w_glu):
    bn, L, _ = u.shape
    uf = u.astype(F32)
    ug = uf.reshape(bn, L, S5_GROUPS, S5_GROUP)
    ldt_re, ldt_im, a_re, a_im, bb_re, bb_im = s5_discretize(lam_re, lam_im, log_step, b_re, b_im)
    bu_re = jnp.einsum('blgh,gph->blgp', ug, bb_re)
    bu_im = jnp.einsum('blgh,gph->blgp', ug, bb_im)
    ar = jnp.broadcast_to(a_re, bu_re.shape)
    ai = jnp.broadcast_to(a_im, bu_re.shape)
    _, _, xr, xi = lax.associative_scan(s5_combine, (ar, ai, bu_re, bu_im), axis=1)
    steps = jnp.arange(1, L + 1, dtype=F32)[:, None, None]
    pm = jnp.exp(ldt_re * steps)
    pr, pi = pm * jnp.cos(ldt_im * steps), pm * jnp.sin(ldt_im * steps)
    x0r, x0i = x0_re.astype(F32)[:, None], x0_im.astype(F32)[:, None]
    xr = xr + pr * x0r - pi * x0i
    xi = xi + pr * x0i + pi * x0r
    y = (jnp.einsum('blgp,ghp->blgh', xr, c_re.astype(F32))
         - jnp.einsum('blgp,ghp->blgh', xi, c_im.astype(F32)))
    y = y.reshape(bn, L, S5_WIDTH) + d * uf
    h = jax.nn.gelu(y)
    out = h * jax.nn.sigmoid(h @ w_glu)
    return out, xr[:, -1], xi[:, -1]


def stick_breaking(q, k, v, bias, q_pos, k_pos):
    z = (jnp.einsum('bqhd,bkhd->bhqk', q, k).astype(F32) * (SB_HEAD_DIM ** -0.5)
         + bias.astype(F32)[None, :, None, None])
    valid = k_pos[None, :] < q_pos[:, None]
    log_beta = jax.nn.log_sigmoid(z)
    log_rest = jnp.where(valid, log_beta - z, 0.0)
    tail = lax.cumsum(log_rest, axis=3, reverse=True) - log_rest
    w = jnp.where(valid, jnp.exp(log_beta + tail), 0.0)
    return jnp.einsum('bhqk,bkhd->bqhd', w, v.astype(F32))


def sb_prompt(q, k, v, bias):
    L = q.shape[1]
    pos = jnp.arange(L)
    outs = [stick_breaking(q[:, lo:lo + SB_BLOCK], k[:, :lo + SB_BLOCK], v[:, :lo + SB_BLOCK], bias,
                           pos[lo:lo + SB_BLOCK], pos[:lo + SB_BLOCK])
            for lo in range(0, L, SB_BLOCK)]
    return jnp.concatenate(outs, axis=1)


def make_sb_sample(k_pool, v_pool, page_table):
    def attend(q, k, v, bias):
        bn, L = q.shape[:2]
        k_past = k_pool[page_table].reshape(bn, -1, SB_HEADS, SB_HEAD_DIM)
        v_past = v_pool[page_table].reshape(bn, -1, SB_HEADS, SB_HEAD_DIM)
        past = k_past.shape[1]
        k_all = jnp.concatenate([k_past.astype(k.dtype), k], axis=1)
        v_all = jnp.concatenate([v_past.astype(v.dtype), v], axis=1)
        return stick_breaking(q, k_all, v_all, bias, past + jnp.arange(L), jnp.arange(past + L))
    return attend


def rope(x, pos):
    half = RET_DK // 2
    inv = ROPE_BASE ** (-jnp.arange(half, dtype=F32) / half)
    ang = pos[:, None] * inv[None, :]
    cos, sin = jnp.cos(ang)[None, :, None, :], jnp.sin(ang)[None, :, None, :]
    x1, x2 = x[..., :half], x[..., half:]
    return jnp.concatenate([x1 * cos - x2 * sin, x1 * sin + x2 * cos], -1)


def ret_log_decay():
    return jnp.log1p(-jnp.exp2(-5.0 - jnp.arange(RET_HEADS, dtype=F32)))


def retention_chunk(s, q, k, v):
    L = q.shape[1]
    lg = ret_log_decay()
    idx = jnp.arange(L, dtype=F32)
    diff = idx[:, None] - idx[None, :]
    decay = jnp.where(diff >= 0, jnp.exp(lg[:, None, None] * jnp.maximum(diff, 0.0)), 0.0)
    inner = jnp.einsum('blhd,bmhd->bhlm', q, k) * decay
    o = jnp.einsum('bhlm,bmhe->blhe', inner, v)
    q_dec = q * jnp.exp(lg[None, :] * (idx[:, None] + 1.0))[None, :, :, None]
    o = o + jnp.einsum('blhd,bhde->blhe', q_dec, s)
    k_dec = k * jnp.exp(lg[None, :] * (L - 1.0 - idx[:, None]))[None, :, :, None]
    s = jnp.exp(lg * L)[None, :, None, None] * s + jnp.einsum('blhd,blhe->bhde', k_dec, v)
    return o, s


def retention_branch(q, k, v, g, pos, s0, gn_g, gn_b):
    bn, L = q.shape[:2]
    q = rope(q.astype(F32).reshape(bn, L, RET_HEADS, RET_DK), pos) * (RET_DK ** -0.5)
    k = rope(k.astype(F32).reshape(bn, L, RET_HEADS, RET_DK), pos)
    v = v.astype(F32).reshape(bn, L, RET_HEADS, RET_DV)
    chunk = RET_CHUNK if L % RET_CHUNK == 0 else L
    n = L // chunk

    def to_chunks(t):
        return t.reshape(bn, n, chunk, *t.shape[2:]).swapaxes(0, 1)

    def step(s, qkv):
        o, s = retention_chunk(s, *qkv)
        return s, o

    s_new, o = lax.scan(step, s0.astype(F32), (to_chunks(q), to_chunks(k), to_chunks(v)))
    o = o.swapaxes(0, 1).reshape(bn, L, RET_HEADS, RET_DV)
    mu = jnp.mean(o, -1, keepdims=True)
    var = jnp.mean(jnp.square(o - mu), -1, keepdims=True)
    o = ((o - mu) * lax.rsqrt(var + LN_EPS)).reshape(bn, L, RET_WIDTH) * gn_g + gn_b
    return jax.nn.silu(g) * o, s_new


def conv_ffn(x, prev, w_up, conv_w, conv_b, w_down):
    L = x.shape[1]
    h = x @ w_up
    hp = jnp.concatenate([prev.astype(h.dtype), h], axis=1)
    c = conv_b + sum(conv_w[j] * hp[:, j:j + L] for j in range(CONV_W))
    a, gate = jnp.split(c, 2, axis=-1)
    return (jax.nn.silu(gate) * a) @ w_down, hp[:, -(CONV_W - 1):]


def trunk_layer(x, pos, s5_re0, s5_im0, ret_s0, conv_prev, sb_attend, p):
    bn, L, _ = x.shape
    h = x @ p['w_in']
    u_a, q_b, k_b, v_b, q_c, k_c, v_c, g_c = jnp.split(h, list(np.cumsum(SPLITS)[:-1]), axis=-1)
    ya, s5r, s5i = s5_branch(u_a, s5_re0, s5_im0, p['lam_re'], p['lam_im'], p['log_step'],
                             p['b_re'], p['b_im'], p['c_re'], p['c_im'], p['d'], p['w_glu'])
    ya = rms_norm(ya, p['norm_a_g'])
    kb = k_b.reshape(bn, L, SB_HEADS, SB_HEAD_DIM)
    vb = v_b.reshape(bn, L, SB_HEADS, SB_HEAD_DIM)
    yb = sb_attend(q_b.reshape(bn, L, SB_HEADS, SB_HEAD_DIM), kb, vb, p['sb_bias']).reshape(bn, L, SB_WIDTH)
    yb = rms_norm(yb, p['norm_b_g'])
    yc, ret_s = retention_branch(q_c, k_c, v_c, g_c, pos, ret_s0, p['gn_g'], p['gn_b'])
    mix = jnp.concatenate([ya, yb, yc], axis=-1).astype(x.dtype) @ p['w_out']
    x = layer_norm(ALPHA * x + mix, p['ln1_g'], p['ln1_b'])
    f, conv_new = conv_ffn(x, conv_prev, p['w_up'], p['conv_w'], p['conv_b'], p['w_down'])
    x = layer_norm(ALPHA * x + f, p['ln2_g'], p['ln2_b'])
    return x, (kb, vb, s5r, s5i, ret_s, conv_new)


def setup_inputs(seed: int = 0) -> dict:
    key = jax.random.key(seed)
    ks = jax.random.split(key, 40)
    ctr = [0]

    def nxt():
        k = ks[ctr[0]]
        ctr[0] += 1
        return k

    def nrm(shape, scale):
        return scale * jax.random.normal(nxt(), shape, F32)

    n_pages = PAST_LEN // PAGE_SIZE
    n_used = DEC_BATCH * n_pages
    n_pool = n_used + (n_used + 3) // 4
    G, P = S5_GROUPS, S5_STATE
    inp = {}
    inp['x_prompt'] = nrm((BATCH, SEQ, D_MODEL), 1.0)
    inp['x_sample'] = nrm((DEC_BATCH, DEC_SEQ, D_MODEL), 1.0)
    inp['cache_k'] = nrm((DEPTH, n_pool, PAGE_SIZE, SB_HEADS, SB_HEAD_DIM), 1.0)
    inp['cache_v'] = nrm((DEPTH, n_pool, PAGE_SIZE, SB_HEADS, SB_HEAD_DIM), 1.0)
    inp['state_s5_re'] = nrm((DEPTH, DEC_BATCH, G, P), 1.0)
    inp['state_s5_im'] = nrm((DEPTH, DEC_BATCH, G, P), 1.0)
    inp['state_ret'] = nrm((DEPTH, DEC_BATCH, RET_HEADS, RET_DK, RET_DV), 1.0)
    inp['state_conv'] = nrm((DEPTH, DEC_BATCH, CONV_W - 1, 2 * D_FF), 1.0)
    inp['page_table'] = jax.random.permutation(nxt(), n_pool)[:n_used].reshape(DEC_BATCH, n_pages).astype(jnp.int32)
    inp['ln_in_g'] = 1.0 + nrm((D_MODEL,), 0.01)
    inp['ln_in_b'] = nrm((D_MODEL,), 0.01)
    inp['w_in'] = nrm((DEPTH, D_MODEL, D_IN), D_MODEL ** -0.5)
    inp['s5_lambda_re'] = -0.5 + nrm((DEPTH, G, P), 0.01)
    inp['s5_lambda_im'] = math.pi * jnp.arange(P, dtype=F32) + nrm((DEPTH, G, P), 0.01)
    inp['s5_log_step'] = jax.random.uniform(nxt(), (DEPTH, G), F32, math.log(1e-3), math.log(1e-1))
    inp['s5_b_re'] = nrm((DEPTH, G, P, S5_GROUP), (2.0 * S5_GROUP) ** -0.5)
    inp['s5_b_im'] = nrm((DEPTH, G, P, S5_GROUP), (2.0 * S5_GROUP) ** -0.5)
    inp['s5_c_re'] = nrm((DEPTH, G, S5_GROUP, P), (2.0 * P) ** -0.5)
    inp['s5_c_im'] = nrm((DEPTH, G, S5_GROUP, P), (2.0 * P) ** -0.5)
    inp['s5_d'] = nrm((DEPTH, S5_WIDTH), 1.0)
    inp['s5_w_glu'] = nrm((DEPTH, S5_WIDTH, S5_WIDTH), S5_WIDTH ** -0.5)
    inp['norm_a_g'] = 1.0 + nrm((DEPTH, S5_WIDTH), 0.01)
    inp['norm_b_g'] = 1.0 + nrm((DEPTH, SB_WIDTH), 0.01)
    inp['sb_logit_bias'] = SB_BIAS_INIT + nrm((DEPTH, SB_HEADS), 0.1)
    inp['ret_gn_g'] = 1.0 + nrm((DEPTH, RET_WIDTH), 0.01)
    inp['ret_gn_b'] = nrm((DEPTH, RET_WIDTH), 0.01)
    inp['w_out'] = nrm((DEPTH, D_MODEL, D_MODEL), BETA * D_MODEL ** -0.5)
    inp['ln1_g'] = 1.0 + nrm((DEPTH, D_MODEL), 0.01)
    inp['ln1_b'] = nrm((DEPTH, D_MODEL), 0.01)
    inp['ffn_w_up'] = nrm((DEPTH, D_MODEL, 2 * D_FF), D_MODEL ** -0.5)
    inp['ffn_conv_w'] = nrm((DEPTH, CONV_W, 2 * D_FF), CONV_W ** -0.5)
    inp['ffn_conv_b'] = nrm((DEPTH, 2 * D_FF), 0.01)
    inp['ffn_w_down'] = nrm((DEPTH, D_FF, D_MODEL), BETA * D_FF ** -0.5)
    inp['ln2_g'] = 1.0 + nrm((DEPTH, D_MODEL), 0.01)
    inp['ln2_b'] = nrm((DEPTH, D_MODEL), 0.01)
    return inp


def reference(x_prompt, x_sample, cache_k, cache_v, state_s5_re, state_s5_im, state_ret, state_conv,
              page_table, ln_in_g, ln_in_b, w_in, s5_lambda_re, s5_lambda_im, s5_log_step,
              s5_b_re, s5_b_im, s5_c_re, s5_c_im, s5_d, s5_w_glu, norm_a_g, norm_b_g, sb_logit_bias,
              ret_gn_g, ret_gn_b, w_out, ln1_g, ln1_b, ffn_w_up, ffn_conv_w, ffn_conv_b,
              ffn_w_down, ln2_g, ln2_b):
    xp = layer_norm(x_prompt, ln_in_g, ln_in_b)
    xs = layer_norm(x_sample, ln_in_g, ln_in_b)
    bp = xp.shape[0]
    pos_p = jnp.arange(xp.shape[1], dtype=F32)
    pos_s = PAST_LEN + jnp.arange(xs.shape[1], dtype=F32)
    zero_s5 = jnp.zeros((bp, S5_GROUPS, S5_STATE), F32)
    zero_ret = jnp.zeros((bp, RET_HEADS, RET_DK, RET_DV), F32)
    zero_conv = jnp.zeros((bp, CONV_W - 1, 2 * D_FF), xp.dtype)
    out_p, out_s = [], []
    for l in range(DEPTH):
        p = dict(w_in=w_in[l], lam_re=s5_lambda_re[l], lam_im=s5_lambda_im[l], log_step=s5_log_step[l],
                 b_re=s5_b_re[l], b_im=s5_b_im[l], c_re=s5_c_re[l], c_im=s5_c_im[l], d=s5_d[l],
                 w_glu=s5_w_glu[l], norm_a_g=norm_a_g[l], norm_b_g=norm_b_g[l], sb_bias=sb_logit_bias[l],
                 gn_g=ret_gn_g[l], gn_b=ret_gn_b[l], w_out=w_out[l], ln1_g=ln1_g[l], ln1_b=ln1_b[l],
                 w_up=ffn_w_up[l], conv_w=ffn_conv_w[l], conv_b=ffn_conv_b[l], w_down=ffn_w_down[l],
                 ln2_g=ln2_g[l], ln2_b=ln2_b[l])
        xp, st_p = trunk_layer(xp, pos_p, zero_s5, zero_s5, zero_ret, zero_conv, sb_prompt, p)
        xs, st_s = trunk_layer(xs, pos_s, state_s5_re[l], state_s5_im[l], state_ret[l], state_conv[l],
                               make_sb_sample(cache_k[l], cache_v[l], page_table), p)
        out_p.append(st_p)
        out_s.append(st_s)
    stk = lambda outs, i: jnp.stack([o[i] for o in outs], axis=0)
    return (xp, xs,
            stk(out_p, 0), stk(out_p, 1), stk(out_s, 0), stk(out_s, 1),
            stk(out_p, 2), stk(out_p, 3), stk(out_s, 2), stk(out_s, 3),
            stk(out_p, 4), stk(out_s, 4),
            stk(out_p, 5), stk(out_s, 5))
```

```python
import functools
import math

import jax
import jax.numpy as jnp
from jax import lax
from jax.experimental import pallas as pl
from jax.experimental.pallas import tpu as pltpu

F32 = jnp.float32
BF16 = jnp.bfloat16

D_MODEL = 1024
DEPTH = 4
PAGE_SIZE = 128
S5_WIDTH = 256
S5_GROUP = 16
S5_GROUPS = 16
S5_STATE = 64
S5_FLAT = S5_GROUPS * S5_STATE
SB_HEADS = 4
SB_HEAD_DIM = 64
SB_WIDTH = 256
RET_WIDTH = 512
RET_HEADS = 4
RET_DV = 128
RET_DK = 64
RET_QK = 256
ROPE_BASE = 10000.0
D_FF = 2816
CONV_W = 3
LN_EPS = 1e-5
RMS_EPS = 1e-6
ALPHA = (2.0 * DEPTH) ** 0.25
SPLITS = (S5_WIDTH, SB_WIDTH, SB_WIDTH, SB_WIDTH, RET_QK, RET_QK, RET_WIDTH, RET_WIDTH)
D_IN = sum(SPLITS)
RET_LOG_DECAY = tuple(math.log1p(-(2.0 ** (-5.0 - h))) for h in range(RET_HEADS))

VMEM_LIMIT = 56 * 1024 * 1024


def _cparams(*sem):
    return pltpu.CompilerParams(dimension_semantics=sem, vmem_limit_bytes=VMEM_LIMIT)


def _dot(a, b):
    return jnp.dot(a, b, preferred_element_type=F32)


def _dot_nt(a, b):
    return lax.dot_general(a, b, (((1,), (1,)), ((), ())), preferred_element_type=F32)


def _dot_tn(a, b):
    return lax.dot_general(a, b, (((0,), (0,)), ((), ())), preferred_element_type=F32)


def _layer_norm(x, g, b):
    mu = jnp.mean(x, -1, keepdims=True)
    xc = x - mu
    var = jnp.mean(xc * xc, -1, keepdims=True)
    return xc * lax.rsqrt(var + LN_EPS) * g + b


def _rms_norm(x, g):
    return x * lax.rsqrt(jnp.mean(x * x, -1, keepdims=True) + RMS_EPS) * g


def _sigmoid(x):
    return 1.0 / (1.0 + jnp.exp(-x))


def _gelu_tanh(x):
    c = math.sqrt(2.0 / math.pi)
    return 0.5 * x * (1.0 + jnp.tanh(c * (x + 0.044715 * (x * x * x))))


def _full(shape):
    return pl.BlockSpec(shape, lambda *_: (0,) * len(shape))


def _rows(tl, w):
    return pl.BlockSpec((None, tl, w), lambda b, t: (b, t, 0))


def _ln_kernel(x_ref, g_ref, b_ref, o_ref):
    o_ref[...] = _layer_norm(x_ref[...], g_ref[...], b_ref[...])


def _input_ln(x, g, b, tl):
    bn, L, d = x.shape
    return pl.pallas_call(
        _ln_kernel,
        grid=(bn, L // tl),
        in_specs=[_rows(tl, d), _full((1, d)), _full((1, d))],
        out_specs=_rows(tl, d),
        out_shape=jax.ShapeDtypeStruct(x.shape, F32),
        compiler_params=_cparams("parallel", "parallel"),
        name="input_ln",
    )(x, g.reshape(1, d), b.reshape(1, d))


def _rope_table_kernel(pos_ref, inv_ref, cos_ref, sin_ref):
    ang = pos_ref[...] * inv_ref[...]
    lane = lax.broadcasted_iota(jnp.int32, ang.shape, 1)
    first = (lane % RET_DK) < (RET_DK // 2)
    cos_ref[...] = jnp.cos(ang)
    s = jnp.sin(ang)
    sin_ref[...] = jnp.where(first, -s, s)


def _rope_tables(pos):
    n = pos.shape[0]
    half = RET_DK // 2
    inv = ROPE_BASE ** (-jnp.arange(half, dtype=F32) / half)
    inv = jnp.tile(inv, 2 * RET_HEADS).reshape(1, RET_QK)
    return pl.pallas_call(
        _rope_table_kernel,
        out_shape=(jax.ShapeDtypeStruct((n, RET_QK), F32),) * 2,
        name="rope_tables",
    )(pos.reshape(n, 1), inv)


def _rope(x, cosf, sinf):
    lane = lax.broadcasted_iota(jnp.int32, (1, 128), 1)
    first = (lane % RET_DK) < (RET_DK // 2)
    halves = []
    for i in range(2):
        sl = slice(i * 128, (i + 1) * 128)
        xh = x[:, sl]
        partner = jnp.where(first, pltpu.roll(xh, 96, 1), pltpu.roll(xh, 32, 1))
        halves.append(xh * cosf[:, sl] + partner * sinf[:, sl])
    return jnp.concatenate(halves, axis=1)


def _s5_disc_kernel(lre_ref, lim_ref, ls_ref, bre_ref, bim_ref, are_ref, aim_ref, bbre_ref, bbim_ref):
    lam_re = jnp.minimum(lre_ref[...], -1e-4)
    lam_im = lim_ref[...]
    dt = jnp.exp(ls_ref[...])
    ldt_re, ldt_im = lam_re * dt, lam_im * dt
    mag = jnp.exp(ldt_re)
    a_re, a_im = mag * jnp.cos(ldt_im), mag * jnp.sin(ldt_im)
    den = lam_re * lam_re + lam_im * lam_im
    n_re, n_im = a_re - 1.0, a_im
    f_re = (n_re * lam_re + n_im * lam_im) / den
    f_im = (n_im * lam_re - n_re * lam_im) / den
    b_re, b_im = bre_ref[...], bim_ref[...]
    are_ref[...] = a_re
    aim_ref[...] = a_im
    bbre_ref[...] = f_re * b_re - f_im * b_im
    bbim_ref[...] = f_re * b_im + f_im * b_re


def _s5_params(lam_re, lam_im, log_step, b_re, b_im, c_re, c_im):
    G, P, H = S5_GROUPS, S5_STATE, S5_GROUP
    col = lambda a: a.reshape(G * P, 1)
    ls = jnp.broadcast_to(log_step[:, None], (G, P))
    a_re, a_im, bb_re, bb_im = pl.pallas_call(
        _s5_disc_kernel,
        out_shape=(jax.ShapeDtypeStruct((G * P, 1), F32),) * 2 + (jax.ShapeDtypeStruct((G * P, H), F32),) * 2,
        name="s5_discretize",
    )(col(lam_re), col(lam_im), col(ls), b_re.reshape(G * P, H), b_im.reshape(G * P, H))
    eye = jnp.eye(G, dtype=F32)

    def bdiag(bb):
        t = bb.reshape(G, P, H).transpose(0, 2, 1)
        return (t[:, :, None, :] * eye[:, None, :, None]).reshape(G * H, G * P)

    def cdiag(c):
        t = c.transpose(0, 2, 1)
        return (t[:, :, None, :] * eye[:, None, :, None]).reshape(G * P, G * H)

    bd = jnp.concatenate([bdiag(bb_re), bdiag(bb_im)], axis=1).astype(BF16)
    cd = jnp.concatenate([cdiag(c_re), -cdiag(c_im)], axis=0).astype(BF16)
    return a_re.reshape(1, G * P), a_im.reshape(1, G * P), bd, cd


KV_OFF = S5_WIDTH + SB_WIDTH


def _inproj_kernel(x_ref, w_ref, wkvt_ref, *out_refs):
    xb = x_ref[...].astype(BF16)
    off = 0
    for i, (ref, wd) in enumerate(zip(out_refs, SPLITS)):
        if i == 2:
            kvt = _dot_nt(wkvt_ref[...], xb)
            out_refs[2][...] = kvt[0:SB_WIDTH]
            out_refs[3][...] = kvt[SB_WIDTH:2 * SB_WIDTH]
        elif i != 3:
            ref[...] = _dot(xb, w_ref[:, off:off + wd])
        off += wd


def _inproj(x, w_bf16, tl):
    bn, L, d = x.shape
    wkvt = w_bf16[:, KV_OFF:KV_OFF + 2 * SB_WIDTH].T
    out_specs, out_shape = [], []
    for i, wd in enumerate(SPLITS):
        if i in (2, 3):
            out_specs.append(pl.BlockSpec((None, wd, tl), lambda b, t: (b, 0, t)))
            out_shape.append(jax.ShapeDtypeStruct((bn, wd, L), F32))
        else:
            out_specs.append(_rows(tl, wd))
            out_shape.append(jax.ShapeDtypeStruct((bn, L, wd), F32))
    return pl.pallas_call(
        _inproj_kernel,
        grid=(bn, L // tl),
        in_specs=[_rows(tl, d), _full((d, D_IN)), _full((2 * SB_WIDTH, d))],
        out_specs=out_specs,
        out_shape=out_shape,
        compiler_params=_cparams("parallel", "parallel"),
        name="inproj",
    )(x, w_bf16, wkvt)


def _s5_kernel(u_ref, bd_ref, cd_ref, are_ref, aim_ref, d_ref, wglu_ref, g_ref, x0r_ref, x0i_ref,
               ya_ref, xr_out, xi_out, bu_scr, xs_scr, st_re, st_im, *, steps, bs):
    c = pl.program_id(0)

    @pl.when(c == 0)
    def _():
        st_re[...] = x0r_ref[...]
        st_im[...] = x0i_ref[...]

    u = u_ref[...]
    bu_scr[...] = _dot(u.astype(BF16), bd_ref[...])
    a_re = jnp.broadcast_to(are_ref[...], (bs, S5_FLAT))
    a_im = jnp.broadcast_to(aim_ref[...], (bs, S5_FLAT))

    def step(t, carry):
        xr, xi = carry
        r0 = pl.multiple_of(t * bs, bs)
        br = bu_scr[pl.ds(r0, bs), 0:S5_FLAT]
        bi = bu_scr[pl.ds(r0, bs), S5_FLAT:2 * S5_FLAT]
        nr = a_re * xr - a_im * xi + br
        ni = a_re * xi + a_im * xr + bi
        xs_scr[pl.ds(r0, bs), 0:S5_FLAT] = nr
        xs_scr[pl.ds(r0, bs), S5_FLAT:2 * S5_FLAT] = ni
        return nr, ni

    xr, xi = lax.fori_loop(0, steps, step, (st_re[...], st_im[...]))
    st_re[...] = xr
    st_im[...] = xi
    xr_out[...] = xr
    xi_out[...] = xi

    y = _dot(xs_scr[...].astype(BF16), cd_ref[...]) + d_ref[...] * u
    h = _gelu_tanh(y)
    out = h * _sigmoid(_dot(h.astype(BF16), wglu_ref[...]))
    ya_ref[...] = _rms_norm(out, g_ref[...])


def _s5_branch(u_tm, x0_re, x0_im, a_re, a_im, bd, cd, d, w_glu, g, steps):
    bs = x0_re.shape[0]
    rows = u_tm.shape[0]
    n_chunks = rows // (steps * bs)
    blk = steps * bs
    kern = functools.partial(_s5_kernel, steps=steps, bs=bs)
    vec = lambda n: _full((1, n))
    return pl.pallas_call(
        kern,
        grid=(n_chunks,),
        in_specs=[pl.BlockSpec((blk, S5_WIDTH), lambda c: (c, 0)),
                  _full(bd.shape), _full(cd.shape), vec(S5_FLAT), vec(S5_FLAT), vec(S5_WIDTH),
                  _full((S5_WIDTH, S5_WIDTH)), vec(S5_WIDTH), _full((bs, S5_FLAT)), _full((bs, S5_FLAT))],
        out_specs=[pl.BlockSpec((blk, S5_WIDTH), lambda c: (c, 0)), _full((bs, S5_FLAT)), _full((bs, S5_FLAT))],
        out_shape=[jax.ShapeDtypeStruct((rows, S5_WIDTH), F32),
                   jax.ShapeDtypeStruct((bs, S5_FLAT), F32), jax.ShapeDtypeStruct((bs, S5_FLAT), F32)],
        scratch_shapes=[pltpu.VMEM((blk, 2 * S5_FLAT), F32), pltpu.VMEM((blk, 2 * S5_FLAT), F32),
                        pltpu.VMEM((bs, S5_FLAT), F32), pltpu.VMEM((bs, S5_FLAT), F32)],
        compiler_params=_cparams("arbitrary"),
        name="s5_branch",
    )(u_tm, bd, cd, a_re, a_im, d.reshape(1, -1), w_glu, g.reshape(1, -1), x0_re, x0_im)


def _sb_logits(z):
    t = jnp.log1p(jnp.exp(-jnp.abs(z)))
    lb = jnp.minimum(z, 0.0) - t
    return lb, lb - z


def _sb_prompt_kernel(bias_ref, q_ref, k_ref, v_ref, g_ref, o_ref, kb_scr, vb_scr, acc_scr, *, tq):
    qi = pl.program_id(1)

    @pl.when(qi == 0)
    def _():
        for jj in range(kb_scr.shape[0]):
            kb_scr[jj] = k_ref[:, jj * tq:(jj + 1) * tq].astype(BF16)
            vb_scr[jj] = v_ref[:, jj * tq:(jj + 1) * tq].astype(BF16)

    q = q_ref[...] * (SB_HEAD_DIM ** -0.5)
    lane = lax.broadcasted_iota(jnp.int32, (1, SB_WIDTH), 1)
    row = lax.broadcasted_iota(jnp.int32, (tq, tq), 0)
    col = lax.broadcasted_iota(jnp.int32, (tq, tq), 1)
    causal = col < row
    later = (row > col).astype(BF16)
    out = jnp.zeros((tq, SB_WIDTH), F32)

    for h in range(SB_HEADS):
        hm = (lane >= h * SB_HEAD_DIM) & (lane < (h + 1) * SB_HEAD_DIM)
        qh = jnp.where(hm, q, 0.0).astype(BF16)
        bias = bias_ref[h]

        def tile(j, carry, masked):
            z = _dot(qh, kb_scr[j]) + bias
            lb, lr = _sb_logits(z)
            if masked:
                lr = jnp.where(causal, lr, 0.0)
            tail = _dot(lr.astype(BF16), later) + carry
            w = jnp.exp(lb + tail)
            if masked:
                w = jnp.where(causal, w, 0.0)
            acc_scr[...] += _dot_nt(w.astype(BF16), vb_scr[j])
            return carry + jnp.sum(lr, axis=1, keepdims=True)

        acc_scr[...] = jnp.zeros_like(acc_scr)
        carry = tile(qi, jnp.zeros((tq, 1), F32), True)
        lax.fori_loop(0, qi, lambda i, c: tile(qi - 1 - i, c, False), carry)
        out = jnp.where(hm, acc_scr[...], out)

    o_ref[...] = _rms_norm(out, g_ref[...])


def _sb_prompt(q, kt, vt, bias, g, tq):
    bn, L, w = q.shape
    kern = functools.partial(_sb_prompt_kernel, tq=tq)
    seq = pl.BlockSpec((None, w, L), lambda b, t: (b, 0, 0))
    return pl.pallas_call(
        kern,
        grid=(bn, L // tq),
        in_specs=[pl.BlockSpec(memory_space=pltpu.SMEM), _rows(tq, w), seq, seq, _full((1, w))],
        out_specs=_rows(tq, w),
        out_shape=jax.ShapeDtypeStruct((bn, L, w), F32),
        scratch_shapes=[pltpu.VMEM((L // tq, w, tq), BF16), pltpu.VMEM((L // tq, w, tq), BF16),
                        pltpu.VMEM((tq, w), F32)],
        compiler_params=_cparams("parallel", "arbitrary"),
        name="sb_prompt",
    )(bias, q, kt, vt, g.reshape(1, w))


def _sb_sample_kernel(pt_ref, bias_ref, q_ref, *refs, n_pages):
    del pt_ref
    k_refs, v_refs = refs[:n_pages], refs[n_pages:2 * n_pages]
    g_ref, o_ref = refs[2 * n_pages], refs[2 * n_pages + 1]
    sub = lax.broadcasted_iota(jnp.int32, (8, SB_WIDTH), 0)
    lane = lax.broadcasted_iota(jnp.int32, (8, SB_WIDTH), 1)
    diag = (lane // SB_HEAD_DIM) == sub
    q = q_ref[...] * (SB_HEAD_DIM ** -0.5)
    qblk = jnp.where(diag, jnp.broadcast_to(q, (8, SB_WIDTH)), 0.0).astype(BF16)
    sub1 = lax.broadcasted_iota(jnp.int32, (8, 1), 0)
    bias = jnp.zeros((8, 1), F32)
    for h in range(SB_HEADS):
        bias = jnp.where(sub1 == h, bias_ref[h], bias)

    lbs, lrs = [], []
    for p in range(n_pages):
        z = _dot(qblk, k_refs[p][...].astype(BF16)) + bias
        lb, lr = _sb_logits(z)
        lbs.append(lb)
        lrs.append(lr)

    r = lax.broadcasted_iota(jnp.int32, (PAGE_SIZE, PAGE_SIZE), 0)
    c = lax.broadcasted_iota(jnp.int32, (PAGE_SIZE, PAGE_SIZE), 1)
    later = (r > c).astype(BF16)
    lr_all = jnp.concatenate(lrs, axis=0)
    hi = lr_all.astype(BF16)
    lo = (lr_all - hi.astype(F32)).astype(BF16)
    tail_in_page = _dot(hi, later) + _dot(lo, later)

    carry = jnp.zeros((8, 1), F32)
    acc = jnp.zeros((8, SB_WIDTH), F32)
    for p in reversed(range(n_pages)):
        w = jnp.exp(lbs[p] + tail_in_page[8 * p:8 * p + 8] + carry)
        acc = acc + _dot_nt(w.astype(BF16), v_refs[p][...].astype(BF16))
        carry = carry + jnp.sum(lrs[p], axis=1, keepdims=True)
    out = jnp.sum(jnp.where(diag, acc, 0.0), axis=0, keepdims=True)
    o_ref[...] = _rms_norm(out, g_ref[...])


def _sb_sample(q, cache_k, cache_v, layer, page_table, bias, g):
    bn = q.shape[0]
    n_pages = page_table.shape[1]
    kern = functools.partial(_sb_sample_kernel, n_pages=n_pages)

    def page_spec(p):
        return pl.BlockSpec((None, None, SB_WIDTH, PAGE_SIZE), lambda b, pt: (layer, pt[b, p], 0, 0))

    row = pl.BlockSpec((None, 1, SB_WIDTH), lambda b, pt: (b, 0, 0))
    grid_spec = pltpu.PrefetchScalarGridSpec(
        num_scalar_prefetch=1,
        grid=(bn,),
        in_specs=[pl.BlockSpec(memory_space=pltpu.SMEM), row]
        + [page_spec(p) for p in range(n_pages)] * 2
        + [pl.BlockSpec((1, SB_WIDTH), lambda b, pt: (0, 0))],
        out_specs=row,
    )
    return pl.pallas_call(
        kern,
        grid_spec=grid_spec,
        out_shape=jax.ShapeDtypeStruct((bn, 1, SB_WIDTH), F32),
        compiler_params=_cparams("arbitrary"),
        name="sb_sample",
    )(page_table, bias, q, *([cache_k] * n_pages), *([cache_v] * n_pages), g.reshape(1, -1))


def _head_lane_consts(width, per_head):
    lane = lax.broadcasted_iota(jnp.int32, (1, width), 1)
    lg = jnp.zeros((1, width), F32)
    for h in range(RET_HEADS):
        lg = jnp.where(lane // per_head == h, RET_LOG_DECAY[h], lg)
    return lg


def _group_norm_gate(o, g, gn_g, gn_b):
    parts = []
    for h in range(RET_HEADS):
        oh = o[:, h * RET_DV:(h + 1) * RET_DV]
        mu = jnp.mean(oh, -1, keepdims=True)
        oc = oh - mu
        var = jnp.mean(oc * oc, -1, keepdims=True)
        parts.append(oc * lax.rsqrt(var + LN_EPS))
    on = jnp.concatenate(parts, axis=1) * gn_g + gn_b
    return g * _sigmoid(g) * on


def _ret_prompt_kernel(q_ref, k_ref, v_ref, g_ref, cos_ref, sin_ref, gng_ref, gnb_ref, s0_ref,
                       o_ref, sout_ref, s_scr, dec_scr, gam_scr, *, ck):
    c = pl.program_id(1)
    rowi = lax.broadcasted_iota(jnp.int32, (RET_QK, RET_WIDTH), 0)
    coli = lax.broadcasted_iota(jnp.int32, (RET_QK, RET_WIDTH), 1)
    blockmask = (rowi // RET_DK) == (coli // RET_DV)

    @pl.when(c == 0)
    def _():
        s_scr[...] = jnp.zeros_like(s_scr)
        di = lax.broadcasted_iota(jnp.int32, (ck, ck), 0) - lax.broadcasted_iota(jnp.int32, (ck, ck), 1)
        df = jnp.maximum(di, 0).astype(F32)
        gam = jnp.zeros((RET_QK, RET_WIDTH), F32)
        for h in range(RET_HEADS):
            s_scr[h * RET_DK:(h + 1) * RET_DK, h * RET_DV:(h + 1) * RET_DV] = s0_ref[h]
            dec_scr[h] = jnp.where(di >= 0, jnp.exp(RET_LOG_DECAY[h] * df), 0.0)
            gam = jnp.where((rowi // RET_DK == h) & blockmask, math.exp(RET_LOG_DECAY[h] * ck), gam)
        gam_scr[...] = gam

    cosf, sinf = cos_ref[...], sin_ref[...]
    q = _rope(q_ref[...], cosf, sinf) * (RET_DK ** -0.5)
    k = _rope(k_ref[...], cosf, sinf)
    v = v_ref[...]
    kb, vb = k.astype(BF16), v.astype(BF16)
    lane = lax.broadcasted_iota(jnp.int32, (1, RET_QK), 1)
    idx = lax.broadcasted_iota(jnp.int32, (ck, 1), 0).astype(F32)
    lg = _head_lane_consts(RET_QK, RET_DK)

    intra = []
    for h in range(RET_HEADS):
        hm = (lane // RET_DK) == h
        inner = _dot_nt(jnp.where(hm, q, 0.0).astype(BF16), kb)
        p = (inner * dec_scr[h]).astype(BF16)
        intra.append(_dot(p, vb[:, h * RET_DV:(h + 1) * RET_DV]))
    q_dec = q * jnp.exp(lg * (idx + 1.0))
    o = jnp.concatenate(intra, axis=1) + _dot(q_dec.astype(BF16), s_scr[...].astype(BF16))
    k_dec = k * jnp.exp(lg * (ck - 1.0 - idx))
    kv = _dot_tn(k_dec.astype(BF16), vb)
    s_new = gam_scr[...] * s_scr[...] + jnp.where(blockmask, kv, 0.0)
    s_scr[...] = s_new
    for h in range(RET_HEADS):
        sout_ref[h] = s_new[h * RET_DK:(h + 1) * RET_DK, h * RET_DV:(h + 1) * RET_DV]
    o_ref[...] = _group_norm_gate(o, g_ref[...], gng_ref[...], gnb_ref[...])


def _ret_prompt(q, k, v, g, cosf, sinf, s0, gn_g, gn_b, ck):
    bn, L, _ = q.shape
    kern = functools.partial(_ret_prompt_kernel, ck=ck)
    tab = pl.BlockSpec((ck, RET_QK), lambda b, c: (c, 0))
    st = pl.BlockSpec((None, RET_HEADS, RET_DK, RET_DV), lambda b, c: (b, 0, 0, 0))
    return pl.pallas_call(
        kern,
        grid=(bn, L // ck),
        in_specs=[_rows(ck, RET_QK), _rows(ck, RET_QK), _rows(ck, RET_WIDTH), _rows(ck, RET_WIDTH),
                  tab, tab, _full((1, RET_WIDTH)), _full((1, RET_WIDTH)), st],
        out_specs=[_rows(ck, RET_WIDTH), st],
        out_shape=[jax.ShapeDtypeStruct((bn, L, RET_WIDTH), F32),
                   jax.ShapeDtypeStruct((bn, RET_HEADS, RET_DK, RET_DV), F32)],
        scratch_shapes=[pltpu.VMEM((RET_QK, RET_WIDTH), F32), pltpu.VMEM((RET_HEADS, ck, ck), F32),
                        pltpu.VMEM((RET_QK, RET_WIDTH), F32)],
        compiler_params=_cparams("parallel", "arbitrary"),
        name="ret_prompt",
    )(q, k, v, g, cosf, sinf, gn_g.reshape(1, -1), gn_b.reshape(1, -1), s0)


def _ret_sample_kernel(q_ref, k_ref, v_ref, g_ref, cos_ref, sin_ref, gng_ref, gnb_ref, s_ref,
                       o_ref, sout_ref, qt_scr, kt_scr, *, gb):
    i = pl.program_id(0)
    nb = q_ref.shape[0]

    @pl.when(i == 0)
    def _():
        cosf = jnp.broadcast_to(cos_ref[...], (nb, RET_QK))
        sinf = jnp.broadcast_to(sin_ref[...], (nb, RET_QK))
        qt_scr[...] = (_rope(q_ref[...], cosf, sinf) * (RET_DK ** -0.5)).T
        kt_scr[...] = _rope(k_ref[...], cosf, sinf).T

    lane = lax.broadcasted_iota(jnp.int32, (1, nb), 1)
    for j in range(gb):
        b = i * gb + j
        onehot = lane == b
        qcol = jnp.sum(jnp.where(onehot, qt_scr[...], 0.0), axis=1, keepdims=True)
        kcol = jnp.sum(jnp.where(onehot, kt_scr[...], 0.0), axis=1, keepdims=True)
        parts = []
        for h in range(RET_HEADS):
            s = s_ref[j, h]
            qc = qcol[h * RET_DK:(h + 1) * RET_DK]
            kc = kcol[h * RET_DK:(h + 1) * RET_DK]
            vrow = v_ref[j:j + 1, h * RET_DV:(h + 1) * RET_DV]
            gamma = math.exp(RET_LOG_DECAY[h])
            qk = jnp.sum(qc * kc, axis=0, keepdims=True)
            parts.append(qk * vrow + jnp.sum((qc * gamma) * s, axis=0, keepdims=True))
            sout_ref[j, h] = gamma * s + kc * vrow
        o = jnp.concatenate(parts, axis=1)
        o_ref[j:j + 1, :] = _group_norm_gate(o, g_ref[j:j + 1, :], gng_ref[...], gnb_ref[...])


def _ret_sample(q, k, v, g, cosf, sinf, s0, gn_g, gn_b, gb=8):
    nb = q.shape[0]
    kern = functools.partial(_ret_sample_kernel, gb=gb)
    st = pl.BlockSpec((gb, RET_HEADS, RET_DK, RET_DV), lambda i: (i, 0, 0, 0))
    return pl.pallas_call(
        kern,
        grid=(nb // gb,),
        in_specs=[_full((nb, RET_QK)), _full((nb, RET_QK)),
                  pl.BlockSpec((gb, RET_WIDTH), lambda i: (i, 0)), pl.BlockSpec((gb, RET_WIDTH), lambda i: (i, 0)),
                  _full((1, RET_QK)), _full((1, RET_QK)), _full((1, RET_WIDTH)), _full((1, RET_WIDTH)), st],
        out_specs=[pl.BlockSpec((gb, RET_WIDTH), lambda i: (i, 0)), st],
        out_shape=[jax.ShapeDtypeStruct((nb, RET_WIDTH), F32),
                   jax.ShapeDtypeStruct((nb, RET_HEADS, RET_DK, RET_DV), F32)],
        scratch_shapes=[pltpu.VMEM((RET_QK, nb), F32), pltpu.VMEM((RET_QK, nb), F32)],
        compiler_params=_cparams("arbitrary"),
        name="ret_sample",
    )(q, k, v, g, cosf, sinf, gn_g.reshape(1, -1), gn_b.reshape(1, -1), s0)


def _outproj_kernel(ya_ref, yb_ref, yc_ref, x_ref, w_ref, g_ref, b_ref, o_ref):
    mix = (_dot(ya_ref[...].astype(BF16), w_ref[0:S5_WIDTH, :])
           + _dot(yb_ref[...].astype(BF16), w_ref[S5_WIDTH:S5_WIDTH + SB_WIDTH, :])
           + _dot(yc_ref[...].astype(BF16), w_ref[S5_WIDTH + SB_WIDTH:, :]))
    o_ref[...] = _layer_norm(ALPHA * x_ref[...] + mix, g_ref[...], b_ref[...])


def _outproj(ya, yb, yc, x, w_bf16, g, b, tl):
    bn, L, d = x.shape
    return pl.pallas_call(
        _outproj_kernel,
        grid=(bn, L // tl),
        in_specs=[_rows(tl, S5_WIDTH), _rows(tl, SB_WIDTH), _rows(tl, RET_WIDTH), _rows(tl, d),
                  _full((d, d)), _full((1, d)), _full((1, d))],
        out_specs=_rows(tl, d),
        out_shape=jax.ShapeDtypeStruct((bn, L, d), F32),
        compiler_params=_cparams("parallel", "parallel"),
        name="outproj_ln",
    )(ya, yb, yc, x, w_bf16, g.reshape(1, d), b.reshape(1, d))


def _conv_taps(h, hm1, hm2, cw_ref, cb_ref):
    return cb_ref[...] + (cw_ref[0:1, :] * hm2 + cw_ref[1:2, :] * hm1 + cw_ref[2:3, :] * h)


def _ffn_prompt_kernel(x_ref, wua_ref, wug_ref, cwa_ref, cwg_ref, cba_ref, cbg_ref, wd_ref, pa_ref, pg_ref,
                       g_ref, b_ref, o_ref, ca_ref, cg_ref, acc_scr, hs_scr, carry_scr, *, tl, nf):
    t = pl.program_id(1)
    c = pl.program_id(2)

    @pl.when(c == 0)
    def _():
        acc_scr[...] = jnp.zeros_like(acc_scr)

    x = x_ref[...]
    xb = x.astype(BF16)

    def branch(part, wu_ref, cw_ref, cb_ref, prev_ref, conv_ref):
        slab = hs_scr.at[part]
        h = _dot(xb, wu_ref[...])

        @pl.when(t == 0)
        def _():
            slab[6:8, :] = prev_ref[...]

        @pl.when(t > 0)
        def _():
            slab[6:8, :] = carry_scr[c, part]

        slab[8:8 + tl, :] = h
        cv = _conv_taps(h, slab[7:7 + tl, :], slab[6:6 + tl, :], cw_ref, cb_ref)
        last2 = slab[tl + 6:tl + 8, :]
        carry_scr[c, part] = last2
        conv_ref[...] = last2
        return cv

    ca = branch(0, wua_ref, cwa_ref, cba_ref, pa_ref, ca_ref)
    cg = branch(1, wug_ref, cwg_ref, cbg_ref, pg_ref, cg_ref)
    act = (cg * _sigmoid(cg) * ca).astype(BF16)
    acc_scr[...] += _dot(act, wd_ref[...])

    @pl.when(c == nf - 1)
    def _():
        o_ref[...] = _layer_norm(ALPHA * x + acc_scr[...], g_ref[...], b_ref[...])


def _ffn_prompt(x, prev, w_up, conv_w, conv_b, w_down, g, b, tl, fc):
    bn, L, d = x.shape
    nf = D_FF // fc
    kern = functools.partial(_ffn_prompt_kernel, tl=tl, nf=nf)
    colblk = lambda rows, off: pl.BlockSpec((rows, fc), lambda bi, t, c: (0, c + off))
    prevblk = lambda off: pl.BlockSpec((None, 2, fc), lambda bi, t, c: (bi, 0, c + off))
    xspec = pl.BlockSpec((None, tl, d), lambda bi, t, c: (bi, t, 0))
    vec = pl.BlockSpec((1, d), lambda bi, t, c: (0, 0))
    convout = pl.BlockSpec((None, None, 2, fc), lambda bi, t, c: (bi, t, 0, c))
    y, ca, cg = pl.pallas_call(
        kern,
        grid=(bn, L // tl, nf),
        in_specs=[xspec, colblk(d, 0), colblk(d, nf), colblk(CONV_W, 0), colblk(CONV_W, nf),
                  colblk(1, 0), colblk(1, nf), pl.BlockSpec((fc, d), lambda bi, t, c: (c, 0)),
                  prevblk(0), prevblk(nf), vec, vec],
        out_specs=[xspec, convout, convout],
        out_shape=[jax.ShapeDtypeStruct((bn, L, d), F32),
                   jax.ShapeDtypeStruct((bn, L // tl, 2, D_FF), F32),
                   jax.ShapeDtypeStruct((bn, L // tl, 2, D_FF), F32)],
        scratch_shapes=[pltpu.VMEM((tl, d), F32), pltpu.VMEM((2, 8 + tl, fc), F32),
                        pltpu.VMEM((nf, 2, 2, fc), F32)],
        compiler_params=_cparams("parallel", "arbitrary", "arbitrary"),
        name="ffn_prompt",
    )(x, w_up, w_up, conv_w, conv_w, conv_b, conv_b, w_down, prev, prev, g.reshape(1, d), b.reshape(1, d))
    return y, jnp.concatenate([ca[:, -1], cg[:, -1]], axis=-1)


def _ffn_sample_kernel(x_ref, wua_ref, wug_ref, cwa_ref, cwg_ref, cba_ref, cbg_ref, wd_ref,
                       p0a_ref, p0g_ref, p1a_ref, p1g_ref, g_ref, b_ref, o_ref, ha_ref, hg_ref, acc_scr, *, nf):
    c = pl.program_id(0)

    @pl.when(c == 0)
    def _():
        acc_scr[...] = jnp.zeros_like(acc_scr)

    x = x_ref[...]
    xb = x.astype(BF16)
    ha = _dot(xb, wua_ref[...])
    hg = _dot(xb, wug_ref[...])
    ha_ref[...] = ha
    hg_ref[...] = hg
    ca = _conv_taps(ha, p1a_ref[...], p0a_ref[...], cwa_ref, cba_ref)
    cg = _conv_taps(hg, p1g_ref[...], p0g_ref[...], cwg_ref, cbg_ref)
    act = (cg * _sigmoid(cg) * ca).astype(BF16)
    acc_scr[...] += _dot(act, wd_ref[...])

    @pl.when(c == nf - 1)
    def _():
        o_ref[...] = _layer_norm(ALPHA * x + acc_scr[...], g_ref[...], b_ref[...])


def _ffn_sample(x, prev, w_up, conv_w, conv_b, w_down, g, b, fc):
    n, d = x.shape
    nf = D_FF // fc
    kern = functools.partial(_ffn_sample_kernel, nf=nf)
    colblk = lambda rows, off: pl.BlockSpec((rows, fc), lambda c: (0, c + off))
    prevblk = lambda j, off: pl.BlockSpec((n, fc), lambda c: (0, j * 2 * nf + c + off))
    prev2d = prev.reshape(n, (CONV_W - 1) * 2 * D_FF)
    full = lambda shape: pl.BlockSpec(shape, lambda c: (0, 0))
    hout = pl.BlockSpec((n, fc), lambda c: (0, c))
    y, ha, hg = pl.pallas_call(
        kern,
        grid=(nf,),
        in_specs=[full((n, d)), colblk(d, 0), colblk(d, nf), colblk(CONV_W, 0), colblk(CONV_W, nf),
                  colblk(1, 0), colblk(1, nf), pl.BlockSpec((fc, d), lambda c: (c, 0)),
                  prevblk(0, 0), prevblk(0, nf), prevblk(1, 0), prevblk(1, nf), full((1, d)), full((1, d))],
        out_specs=[full((n, d)), hout, hout],
        out_shape=[jax.ShapeDtypeStruct((n, d), F32),
                   jax.ShapeDtypeStruct((n, D_FF), F32), jax.ShapeDtypeStruct((n, D_FF), F32)],
        scratch_shapes=[pltpu.VMEM((n, d), F32)],
        compiler_params=_cparams("arbitrary"),
        name="ffn_sample",
    )(x, w_up, w_up, conv_w, conv_w, conv_b, conv_b, w_down, prev2d, prev2d, prev2d, prev2d,
      g.reshape(1, d), b.reshape(1, d))
    h_new = jnp.concatenate([ha, hg], axis=-1)
    return y, jnp.stack([prev[:, 1], h_new], axis=1)


def kernel(x_prompt, x_sample, cache_k, cache_v, state_s5_re, state_s5_im, state_ret, state_conv, page_table, ln_in_g, ln_in_b, w_in, s5_lambda_re, s5_lambda_im, s5_log_step, s5_b_re, s5_b_im, s5_c_re, s5_c_im, s5_d, s5_w_glu, norm_a_g, norm_b_g, sb_logit_bias, ret_gn_g, ret_gn_b, w_out, ln1_g, ln1_b, ffn_w_up, ffn_conv_w, ffn_conv_b, ffn_w_down, ln2_g, ln2_b):
    bp, lp, d = x_prompt.shape
    ns = x_sample.shape[0]
    past = page_table.shape[1] * PAGE_SIZE
    n_pool = cache_k.shape[1]
    ck_pages = cache_k.transpose(0, 1, 3, 4, 2).reshape(DEPTH, n_pool, SB_WIDTH, PAGE_SIZE)
    cv_pages = cache_v.transpose(0, 1, 3, 4, 2).reshape(DEPTH, n_pool, SB_WIDTH, PAGE_SIZE)

    def token_major(t):
        bn, _, L = t.shape
        return t.reshape(bn, SB_HEADS, SB_HEAD_DIM, L).transpose(0, 3, 1, 2)

    xp = _input_ln(x_prompt, ln_in_g, ln_in_b, 1024)
    xs = _input_ln(x_sample.reshape(1, ns, d), ln_in_g, ln_in_b, ns)
    cos_p, sin_p = _rope_tables(jnp.arange(lp, dtype=F32))
    cos_s, sin_s = _rope_tables(jnp.full((8,), past, F32))
    cos_s, sin_s = cos_s[0:1], sin_s[0:1]

    zero_s5 = jnp.zeros((bp, S5_FLAT), F32)
    zero_ret = jnp.zeros((bp, RET_HEADS, RET_DK, RET_DV), F32)
    zero_conv = jnp.zeros((bp, CONV_W - 1, 2 * D_FF), F32)
    outs = [[] for _ in range(12)]

    for l in range(DEPTH):
        w_in_b = w_in[l].astype(BF16)
        w_out_b = w_out[l].astype(BF16)
        w_up_b = ffn_w_up[l].astype(BF16)
        w_down_b = ffn_w_down[l].astype(BF16)
        w_glu_b = s5_w_glu[l].astype(BF16)
        cb = ffn_conv_b[l].reshape(1, -1)
        a_re, a_im, bd, cd = _s5_params(s5_lambda_re[l], s5_lambda_im[l], s5_log_step[l],
                                        s5_b_re[l], s5_b_im[l], s5_c_re[l], s5_c_im[l])

        u, qb, kb, vb, qc, kc, vc, gc = _inproj(xp, w_in_b, 512)
        ya, s5r, s5i = _s5_branch(u.transpose(1, 0, 2).reshape(lp * bp, S5_WIDTH), zero_s5, zero_s5, a_re, a_im, bd, cd,
                                  s5_d[l], w_glu_b, norm_a_g[l], steps=128)
        yb = _sb_prompt(qb, kb, vb, sb_logit_bias[l], norm_b_g[l], 256)
        yc, ret_s = _ret_prompt(qc, kc, vc, gc, cos_p, sin_p, zero_ret, ret_gn_g[l], ret_gn_b[l], 256)
        x1 = _outproj(ya.reshape(lp, bp, S5_WIDTH).transpose(1, 0, 2), yb, yc, xp, w_out_b, ln1_g[l], ln1_b[l], 512)
        xp, conv_p = _ffn_prompt(x1, zero_conv, w_up_b, ffn_conv_w[l], cb, w_down_b, ln2_g[l], ln2_b[l], 512, 1408)
        for i, a in enumerate((token_major(kb), token_major(vb),
                               s5r.reshape(bp, S5_GROUPS, S5_STATE), s5i.reshape(bp, S5_GROUPS, S5_STATE),
                               ret_s, conv_p)):
            outs[i].append(a)

        u, qb, kb, vb, qc, kc, vc, gc = _inproj(xs, w_in_b, ns)
        ya, s5r, s5i = _s5_branch(u.reshape(ns, S5_WIDTH), state_s5_re[l].reshape(ns, S5_FLAT),
                                  state_s5_im[l].reshape(ns, S5_FLAT), a_re, a_im, bd, cd,
                                  s5_d[l], w_glu_b, norm_a_g[l], steps=1)
        yb = _sb_sample(qb.reshape(ns, 1, SB_WIDTH), ck_pages, cv_pages, l, page_table,
                        sb_logit_bias[l], norm_b_g[l])
        yc, ret_s = _ret_sample(qc[0], kc[0], vc[0], gc[0], cos_s, sin_s, state_ret[l],
                                ret_gn_g[l], ret_gn_b[l])
        x1 = _outproj(ya.reshape(1, ns, S5_WIDTH), yb.reshape(1, ns, SB_WIDTH), yc.reshape(1, ns, RET_WIDTH),
                      xs, w_out_b, ln1_g[l], ln1_b[l], ns)
        x2, conv_s = _ffn_sample(x1[0], state_conv[l], w_up_b, ffn_conv_w[l], cb, w_down_b,
                                 ln2_g[l], ln2_b[l], 1408)
        xs = x2.reshape(1, ns, d)
        for i, a in enumerate((token_major(kb).reshape(ns, 1, SB_HEADS, SB_HEAD_DIM),
                               token_major(vb).reshape(ns, 1, SB_HEADS, SB_HEAD_DIM),
                               s5r.reshape(ns, S5_GROUPS, S5_STATE), s5i.reshape(ns, S5_GROUPS, S5_STATE),
                               ret_s, conv_s)):
            outs[6 + i].append(a)

    stk = lambda i: jnp.stack(outs[i], axis=0)
    return (xp, xs.reshape(ns, 1, d),
            stk(0), stk(1), stk(6), stk(7),
            stk(2), stk(3), stk(8), stk(9),
            stk(4), stk(10),
            stk(5), stk(11))
```

```python
import functools
import math

import jax
import jax.numpy as jnp
from jax import lax
from jax.experimental import pallas as pl
from jax.experimental.pallas import tpu as pltpu

F32 = jnp.float32
BF16 = jnp.bfloat16

D_MODEL = 1024
DEPTH = 4
PAGE_SIZE = 128
S5_WIDTH = 256
S5_GROUP = 16
S5_GROUPS = 16
S5_STATE = 64
S5_FLAT = S5_GROUPS * S5_STATE
SB_HEADS = 4
SB_HEAD_DIM = 64
SB_WIDTH = 256
RET_WIDTH = 512
RET_HEADS = 4
RET_DV = 128
RET_DK = 64
RET_QK = 256
ROPE_BASE = 10000.0
D_FF = 2816
CONV_W = 3
LN_EPS = 1e-5
RMS_EPS = 1e-6
ALPHA = (2.0 * DEPTH) ** 0.25
SPLITS = (S5_WIDTH, SB_WIDTH, SB_WIDTH, SB_WIDTH, RET_QK, RET_QK, RET_WIDTH, RET_WIDTH)
D_IN = sum(SPLITS)
RET_LOG_DECAY = tuple(math.log1p(-(2.0 ** (-5.0 - h))) for h in range(RET_HEADS))

VMEM_LIMIT = 56 * 1024 * 1024


def _cparams(*sem):
    return pltpu.CompilerParams(dimension_semantics=sem, vmem_limit_bytes=VMEM_LIMIT)


def _dot(a, b):
    return jnp.dot(a, b, preferred_element_type=F32)


def _dot_nt(a, b):
    return lax.dot_general(a, b, (((1,), (1,)), ((), ())), preferred_element_type=F32)


def _dot_tn(a, b):
    return lax.dot_general(a, b, (((0,), (0,)), ((), ())), preferred_element_type=F32)


def _layer_norm(x, g, b):
    mu = jnp.mean(x, -1, keepdims=True)
    xc = x - mu
    var = jnp.mean(xc * xc, -1, keepdims=True)
    return xc * lax.rsqrt(var + LN_EPS) * g + b


def _rms_norm(x, g):
    return x * lax.rsqrt(jnp.mean(x * x, -1, keepdims=True) + RMS_EPS) * g


def _sigmoid(x):
    return 1.0 / (1.0 + jnp.exp(-x))


def _gelu_tanh(x):
    c = math.sqrt(2.0 / math.pi)
    return 0.5 * x * (1.0 + jnp.tanh(c * (x + 0.044715 * (x * x * x))))


def _full(shape):
    return pl.BlockSpec(shape, lambda *_: (0,) * len(shape))


def _rows(tl, w):
    return pl.BlockSpec((None, tl, w), lambda b, t: (b, t, 0))


def _ln_kernel(x_ref, g_ref, b_ref, o_ref):
    o_ref[...] = _layer_norm(x_ref[...], g_ref[...], b_ref[...])


def _input_ln(x, g, b, tl):
    bn, L, d = x.shape
    return pl.pallas_call(
        _ln_kernel,
        grid=(bn, L // tl),
        in_specs=[_rows(tl, d), _full((1, d)), _full((1, d))],
        out_specs=_rows(tl, d),
        out_shape=jax.ShapeDtypeStruct(x.shape, F32),
        compiler_params=_cparams("parallel", "parallel"),
        name="input_ln",
    )(x, g.reshape(1, d), b.reshape(1, d))


def _rope_table_kernel(pos_ref, inv_ref, cos_ref, sin_ref):
    ang = pos_ref[...] * inv_ref[...]
    lane = lax.broadcasted_iota(jnp.int32, ang.shape, 1)
    first = (lane % RET_DK) < (RET_DK // 2)
    cos_ref[...] = jnp.cos(ang)
    s = jnp.sin(ang)
    sin_ref[...] = jnp.where(first, -s, s)


def _rope_tables(pos):
    n = pos.shape[0]
    half = RET_DK // 2
    inv = ROPE_BASE ** (-jnp.arange(half, dtype=F32) / half)
    inv = jnp.tile(inv, 2 * RET_HEADS).reshape(1, RET_QK)
    return pl.pallas_call(
        _rope_table_kernel,
        out_shape=(jax.ShapeDtypeStruct((n, RET_QK), F32),) * 2,
        name="rope_tables",
    )(pos.reshape(n, 1), inv)


def _rope(x, cosf, sinf):
    lane = lax.broadcasted_iota(jnp.int32, (1, 128), 1)
    first = (lane % RET_DK) < (RET_DK // 2)
    halves = []
    for i in range(2):
        sl = slice(i * 128, (i + 1) * 128)
        xh = x[:, sl]
        partner = jnp.where(first, pltpu.roll(xh, 96, 1), pltpu.roll(xh, 32, 1))
        halves.append(xh * cosf[:, sl] + partner * sinf[:, sl])
    return jnp.concatenate(halves, axis=1)


def _s5_disc_kernel(lre_ref, lim_ref, ls_ref, bre_ref, bim_ref, are_ref, aim_ref, bbre_ref, bbim_ref):
    lam_re = jnp.minimum(lre_ref[...], -1e-4)
    lam_im = lim_ref[...]
    dt = jnp.exp(ls_ref[...])
    ldt_re, ldt_im = lam_re * dt, lam_im * dt
    mag = jnp.exp(ldt_re)
    a_re, a_im = mag * jnp.cos(ldt_im), mag * jnp.sin(ldt_im)
    den = lam_re * lam_re + lam_im * lam_im
    n_re, n_im = a_re - 1.0, a_im
    f_re = (n_re * lam_re + n_im * lam_im) / den
    f_im = (n_im * lam_re - n_re * lam_im) / den
    b_re, b_im = bre_ref[...], bim_ref[...]
    are_ref[...] = a_re
    aim_ref[...] = a_im
    bbre_ref[...] = f_re * b_re - f_im * b_im
    bbim_ref[...] = f_re * b_im + f_im * b_re


def _s5_params(lam_re, lam_im, log_step, b_re, b_im, c_re, c_im):
    G, P, H = S5_GROUPS, S5_STATE, S5_GROUP
    col = lambda a: a.reshape(G * P, 1)
    ls = jnp.broadcast_to(log_step[:, None], (G, P))
    a_re, a_im, bb_re, bb_im = pl.pallas_call(
        _s5_disc_kernel,
        out_shape=(jax.ShapeDtypeStruct((G * P, 1), F32),) * 2 + (jax.ShapeDtypeStruct((G * P, H), F32),) * 2,
        name="s5_discretize",
    )(col(lam_re), col(lam_im), col(ls), b_re.reshape(G * P, H), b_im.reshape(G * P, H))
    eye = jnp.eye(G, dtype=F32)

    def bdiag(bb):
        t = bb.reshape(G, P, H).transpose(0, 2, 1)
        return (t[:, :, None, :] * eye[:, None, :, None]).reshape(G * H, G * P)

    def cdiag(c):
        t = c.transpose(0, 2, 1)
        return (t[:, :, None, :] * eye[:, None, :, None]).reshape(G * P, G * H)

    bd = jnp.concatenate([bdiag(bb_re), bdiag(bb_im)], axis=1).astype(BF16)
    cd = jnp.concatenate([cdiag(c_re), -cdiag(c_im)], axis=0).astype(BF16)
    return a_re.reshape(1, G * P), a_im.reshape(1, G * P), bd, cd


KV_OFF = S5_WIDTH + SB_WIDTH


def _inproj_kernel(x_ref, w_ref, wkvt_ref, kbuf_ref, vbuf_ref, *out_refs):
    del kbuf_ref, vbuf_ref
    xb = x_ref[...].astype(BF16)
    off = 0
    for i, (ref, wd) in enumerate(zip(out_refs, SPLITS)):
        if i == 2:
            kvt = _dot_nt(wkvt_ref[...], xb)
            out_refs[2][...] = kvt[0:SB_WIDTH]
            out_refs[3][...] = kvt[SB_WIDTH:2 * SB_WIDTH]
        elif i != 3:
            ref[...] = _dot(xb, w_ref[:, off:off + wd])
        off += wd


def _inproj(x, w_all, wkvt_all, layer, kt_buf, vt_buf, tl):
    bn, L, d = x.shape
    out_specs, out_shape = [], []
    for i, wd in enumerate(SPLITS):
        if i in (2, 3):
            out_specs.append(pl.BlockSpec((None, None, wd, tl), lambda b, t: (layer, b, 0, t)))
            out_shape.append(jax.ShapeDtypeStruct((DEPTH, bn, wd, L), F32))
        else:
            out_specs.append(_rows(tl, wd))
            out_shape.append(jax.ShapeDtypeStruct((bn, L, wd), F32))
    any_spec = pl.BlockSpec(memory_space=pl.ANY)
    return pl.pallas_call(
        _inproj_kernel,
        grid=(bn, L // tl),
        in_specs=[_rows(tl, d), pl.BlockSpec((None, d, D_IN), lambda b, t: (layer, 0, 0)),
                  pl.BlockSpec((None, 2 * SB_WIDTH, d), lambda b, t: (layer, 0, 0)), any_spec, any_spec],
        out_specs=out_specs,
        out_shape=out_shape,
        input_output_aliases={3: 2, 4: 3},
        compiler_params=_cparams("parallel", "parallel"),
        name="inproj",
    )(x, w_all, wkvt_all, kt_buf, vt_buf)


def _s5_kernel(u_ref, bd_ref, cd_ref, are_ref, aim_ref, d_ref, wglu_ref, g_ref, x0r_ref, x0i_ref,
               ya_ref, xr_out, xi_out, bu_scr, xs_scr, st_re, st_im, *, steps, bs):
    c = pl.program_id(0)

    @pl.when(c == 0)
    def _():
        st_re[...] = x0r_ref[...]
        st_im[...] = x0i_ref[...]

    u = u_ref[...]
    bu_scr[...] = _dot(u.astype(BF16), bd_ref[...])
    a_re = jnp.broadcast_to(are_ref[...], (bs, S5_FLAT))
    a_im = jnp.broadcast_to(aim_ref[...], (bs, S5_FLAT))

    def step(t, carry):
        xr, xi = carry
        r0 = pl.multiple_of(t * bs, bs)
        br = bu_scr[pl.ds(r0, bs), 0:S5_FLAT]
        bi = bu_scr[pl.ds(r0, bs), S5_FLAT:2 * S5_FLAT]
        nr = a_re * xr - a_im * xi + br
        ni = a_re * xi + a_im * xr + bi
        xs_scr[pl.ds(r0, bs), 0:S5_FLAT] = nr
        xs_scr[pl.ds(r0, bs), S5_FLAT:2 * S5_FLAT] = ni
        return nr, ni

    xr, xi = lax.fori_loop(0, steps, step, (st_re[...], st_im[...]))
    st_re[...] = xr
    st_im[...] = xi
    xr_out[...] = xr
    xi_out[...] = xi

    y = _dot(xs_scr[...].astype(BF16), cd_ref[...]) + d_ref[...] * u
    h = _gelu_tanh(y)
    out = h * _sigmoid(_dot(h.astype(BF16), wglu_ref[...]))
    ya_ref[...] = _rms_norm(out, g_ref[...])


def _s5_branch(u_tm, x0_re, x0_im, a_re, a_im, bd, cd, d, w_glu, g, steps):
    bs = x0_re.shape[0]
    rows = u_tm.shape[0]
    n_chunks = rows // (steps * bs)
    blk = steps * bs
    kern = functools.partial(_s5_kernel, steps=steps, bs=bs)
    vec = lambda n: _full((1, n))
    return pl.pallas_call(
        kern,
        grid=(n_chunks,),
        in_specs=[pl.BlockSpec((blk, S5_WIDTH), lambda c: (c, 0)),
                  _full(bd.shape), _full(cd.shape), vec(S5_FLAT), vec(S5_FLAT), vec(S5_WIDTH),
                  _full((S5_WIDTH, S5_WIDTH)), vec(S5_WIDTH), _full((bs, S5_FLAT)), _full((bs, S5_FLAT))],
        out_specs=[pl.BlockSpec((blk, S5_WIDTH), lambda c: (c, 0)), _full((bs, S5_FLAT)), _full((bs, S5_FLAT))],
        out_shape=[jax.ShapeDtypeStruct((rows, S5_WIDTH), F32),
                   jax.ShapeDtypeStruct((bs, S5_FLAT), F32), jax.ShapeDtypeStruct((bs, S5_FLAT), F32)],
        scratch_shapes=[pltpu.VMEM((blk, 2 * S5_FLAT), F32), pltpu.VMEM((blk, 2 * S5_FLAT), F32),
                        pltpu.VMEM((bs, S5_FLAT), F32), pltpu.VMEM((bs, S5_FLAT), F32)],
        compiler_params=_cparams("arbitrary"),
        name="s5_branch",
    )(u_tm, bd, cd, a_re, a_im, d.reshape(1, -1), w_glu, g.reshape(1, -1), x0_re, x0_im)


def _sb_logits(z):
    t = jnp.log(1.0 + jnp.exp(-jnp.abs(z)))
    lb = jnp.minimum(z, 0.0) - t
    return lb, lb - z


def _sb_prompt_kernel(bias_ref, q_ref, k_ref, v_ref, g_ref, o_ref, kb_scr, vb_scr, acc_scr, *, tq):
    qi = pl.program_id(1)

    @pl.when(qi == 0)
    def _():
        for jj in range(kb_scr.shape[0]):
            kb_scr[jj] = k_ref[:, jj * tq:(jj + 1) * tq].astype(BF16)
            vb_scr[jj] = v_ref[:, jj * tq:(jj + 1) * tq].astype(BF16)

    rows = SB_HEADS * tq
    q = q_ref[...] * (SB_HEAD_DIM ** -0.5)
    lane = lax.broadcasted_iota(jnp.int32, (1, SB_WIDTH), 1)
    q4 = jnp.concatenate([jnp.where(lane // SB_HEAD_DIM == h, q, 0.0) for h in range(SB_HEADS)],
                         axis=0).astype(BF16)
    rblock = lax.broadcasted_iota(jnp.int32, (rows, 1), 0) // tq
    bias = jnp.zeros((rows, 1), F32)
    for h in range(SB_HEADS):
        bias = jnp.where(rblock == h, bias_ref[h], bias)
    qpos = lax.broadcasted_iota(jnp.int32, (rows, tq), 0) % tq
    kpos = lax.broadcasted_iota(jnp.int32, (rows, tq), 1)
    causal = kpos < qpos
    later = (lax.broadcasted_iota(jnp.int32, (tq, tq), 0)
             > lax.broadcasted_iota(jnp.int32, (tq, tq), 1)).astype(BF16)

    def tile(j, carry, masked):
        z = _dot(q4, kb_scr[j]) + bias
        lb, lr = _sb_logits(z)
        if masked:
            lr = jnp.where(causal, lr, 0.0)
        tail = _dot(lr.astype(BF16), later) + carry
        w = jnp.exp(lb + tail)
        if masked:
            w = jnp.where(causal, w, 0.0)
        acc_scr[...] += _dot_nt(w.astype(BF16), vb_scr[j])
        return carry + jnp.sum(lr, axis=1, keepdims=True)

    acc_scr[...] = jnp.zeros_like(acc_scr)
    carry = tile(qi, jnp.zeros((rows, 1), F32), True)
    lax.fori_loop(0, qi, lambda i, c: tile(qi - 1 - i, c, False), carry)
    out = jnp.zeros((tq, SB_WIDTH), F32)
    for h in range(SB_HEADS):
        out = jnp.where(lane // SB_HEAD_DIM == h, acc_scr[h * tq:(h + 1) * tq, :], out)
    o_ref[...] = _rms_norm(out, g_ref[...])


def _sb_prompt(q, kt, vt, layer, bias, g, tq):
    bn, L, w = q.shape
    kern = functools.partial(_sb_prompt_kernel, tq=tq)
    seq = pl.BlockSpec((None, None, w, L), lambda b, t: (layer, b, 0, 0))
    return pl.pallas_call(
        kern,
        grid=(bn, L // tq),
        in_specs=[pl.BlockSpec(memory_space=pltpu.SMEM), _rows(tq, w), seq, seq, _full((1, w))],
        out_specs=_rows(tq, w),
        out_shape=jax.ShapeDtypeStruct((bn, L, w), F32),
        scratch_shapes=[pltpu.VMEM((L // tq, w, tq), BF16), pltpu.VMEM((L // tq, w, tq), BF16),
                        pltpu.VMEM((SB_HEADS * tq, w), F32)],
        compiler_params=_cparams("parallel", "arbitrary"),
        name="sb_prompt",
    )(bias, q, kt, vt, g.reshape(1, w))


def _sb_sample_kernel(pt_ref, bias_ref, q_ref, *refs, n_pages):
    del pt_ref
    k_refs, v_refs = refs[:n_pages], refs[n_pages:2 * n_pages]
    g_ref, o_ref = refs[2 * n_pages], refs[2 * n_pages + 1]
    sub = lax.broadcasted_iota(jnp.int32, (8, SB_WIDTH), 0)
    lane = lax.broadcasted_iota(jnp.int32, (8, SB_WIDTH), 1)
    diag = (lane // SB_HEAD_DIM) == sub
    q = q_ref[...] * (SB_HEAD_DIM ** -0.5)
    qblk = jnp.where(diag, jnp.broadcast_to(q, (8, SB_WIDTH)), 0.0).astype(BF16)
    sub1 = lax.broadcasted_iota(jnp.int32, (8, 1), 0)
    bias = jnp.zeros((8, 1), F32)
    for h in range(SB_HEADS):
        bias = jnp.where(sub1 == h, bias_ref[h], bias)

    lbs, lrs = [], []
    for p in range(n_pages):
        z = _dot(qblk, k_refs[p][...].astype(BF16)) + bias
        lb, lr = _sb_logits(z)
        lbs.append(lb)
        lrs.append(lr)

    r = lax.broadcasted_iota(jnp.int32, (PAGE_SIZE, PAGE_SIZE), 0)
    c = lax.broadcasted_iota(jnp.int32, (PAGE_SIZE, PAGE_SIZE), 1)
    later = (r > c).astype(BF16)
    lr_all = jnp.concatenate(lrs, axis=0)
    hi = lr_all.astype(BF16)
    lo = (lr_all - hi.astype(F32)).astype(BF16)
    tail_in_page = _dot(hi, later) + _dot(lo, later)

    carry = jnp.zeros((8, 1), F32)
    acc = jnp.zeros((8, SB_WIDTH), F32)
    for p in reversed(range(n_pages)):
        w = jnp.exp(lbs[p] + tail_in_page[8 * p:8 * p + 8] + carry)
        acc = acc + _dot_nt(w.astype(BF16), v_refs[p][...].astype(BF16))
        carry = carry + jnp.sum(lrs[p], axis=1, keepdims=True)
    out = jnp.sum(jnp.where(diag, acc, 0.0), axis=0, keepdims=True)
    o_ref[...] = _rms_norm(out, g_ref[...])


def _sb_sample(q, cache_k, cache_v, layer, page_table, bias, g):
    bn = q.shape[0]
    n_pages = page_table.shape[1]
    kern = functools.partial(_sb_sample_kernel, n_pages=n_pages)

    def page_spec(p):
        return pl.BlockSpec((None, None, SB_WIDTH, PAGE_SIZE), lambda b, pt: (layer, pt[b, p], 0, 0))

    row = pl.BlockSpec((None, 1, SB_WIDTH), lambda b, pt: (b, 0, 0))
    grid_spec = pltpu.PrefetchScalarGridSpec(
        num_scalar_prefetch=1,
        grid=(bn,),
        in_specs=[pl.BlockSpec(memory_space=pltpu.SMEM), row]
        + [page_spec(p) for p in range(n_pages)] * 2
        + [pl.BlockSpec((1, SB_WIDTH), lambda b, pt: (0, 0))],
        out_specs=row,
    )
    return pl.pallas_call(
        kern,
        grid_spec=grid_spec,
        out_shape=jax.ShapeDtypeStruct((bn, 1, SB_WIDTH), F32),
        compiler_params=_cparams("arbitrary"),
        name="sb_sample",
    )(page_table, bias, q, *([cache_k] * n_pages), *([cache_v] * n_pages), g.reshape(1, -1))


def _head_lane_consts(width, per_head):
    lane = lax.broadcasted_iota(jnp.int32, (1, width), 1)
    lg = jnp.zeros((1, width), F32)
    for h in range(RET_HEADS):
        lg = jnp.where(lane // per_head == h, RET_LOG_DECAY[h], lg)
    return lg


def _group_norm_gate(o, g, gn_g, gn_b):
    parts = []
    for h in range(RET_HEADS):
        oh = o[:, h * RET_DV:(h + 1) * RET_DV]
        mu = jnp.mean(oh, -1, keepdims=True)
        oc = oh - mu
        var = jnp.mean(oc * oc, -1, keepdims=True)
        parts.append(oc * lax.rsqrt(var + LN_EPS))
    on = jnp.concatenate(parts, axis=1) * gn_g + gn_b
    return g * _sigmoid(g) * on


def _ret_prompt_kernel(q_ref, k_ref, v_ref, g_ref, cos_ref, sin_ref, gng_ref, gnb_ref, s0_ref,
                       o_ref, sout_ref, s_scr, dec_scr, gam_scr, *, ck):
    c = pl.program_id(1)
    rowi = lax.broadcasted_iota(jnp.int32, (RET_QK, RET_WIDTH), 0)
    coli = lax.broadcasted_iota(jnp.int32, (RET_QK, RET_WIDTH), 1)
    blockmask = (rowi // RET_DK) == (coli // RET_DV)

    @pl.when(c == 0)
    def _():
        s_scr[...] = jnp.zeros_like(s_scr)
        di = lax.broadcasted_iota(jnp.int32, (ck, ck), 0) - lax.broadcasted_iota(jnp.int32, (ck, ck), 1)
        df = jnp.maximum(di, 0).astype(F32)
        gam = jnp.zeros((RET_QK, RET_WIDTH), F32)
        for h in range(RET_HEADS):
            s_scr[h * RET_DK:(h + 1) * RET_DK, h * RET_DV:(h + 1) * RET_DV] = s0_ref[h]
            dec_scr[h] = jnp.where(di >= 0, jnp.exp(RET_LOG_DECAY[h] * df), 0.0)
            gam = jnp.where((rowi // RET_DK == h) & blockmask, math.exp(RET_LOG_DECAY[h] * ck), gam)
        gam_scr[...] = gam

    cosf, sinf = cos_ref[...], sin_ref[...]
    q = _rope(q_ref[...], cosf, sinf) * (RET_DK ** -0.5)
    k = _rope(k_ref[...], cosf, sinf)
    v = v_ref[...]
    kb, vb = k.astype(BF16), v.astype(BF16)
    lane = lax.broadcasted_iota(jnp.int32, (1, RET_QK), 1)
    idx = lax.broadcasted_iota(jnp.int32, (ck, 1), 0).astype(F32)
    lg = _head_lane_consts(RET_QK, RET_DK)

    intra = []
    for h in range(RET_HEADS):
        hm = (lane // RET_DK) == h
        inner = _dot_nt(jnp.where(hm, q, 0.0).astype(BF16), kb)
        p = (inner * dec_scr[h]).astype(BF16)
        intra.append(_dot(p, vb[:, h * RET_DV:(h + 1) * RET_DV]))
    q_dec = q * jnp.exp(lg * (idx + 1.0))
    o = jnp.concatenate(intra, axis=1) + _dot(q_dec.astype(BF16), s_scr[...].astype(BF16))
    k_dec = k * jnp.exp(lg * (ck - 1.0 - idx))
    kv = _dot_tn(k_dec.astype(BF16), vb)
    s_new = gam_scr[...] * s_scr[...] + jnp.where(blockmask, kv, 0.0)
    s_scr[...] = s_new
    for h in range(RET_HEADS):
        sout_ref[h] = s_new[h * RET_DK:(h + 1) * RET_DK, h * RET_DV:(h + 1) * RET_DV]
    o_ref[...] = _group_norm_gate(o, g_ref[...], gng_ref[...], gnb_ref[...])


def _ret_prompt(q, k, v, g, cosf, sinf, s0, gn_g, gn_b, ck):
    bn, L, _ = q.shape
    kern = functools.partial(_ret_prompt_kernel, ck=ck)
    tab = pl.BlockSpec((ck, RET_QK), lambda b, c: (c, 0))
    st = pl.BlockSpec((None, RET_HEADS, RET_DK, RET_DV), lambda b, c: (b, 0, 0, 0))
    return pl.pallas_call(
        kern,
        grid=(bn, L // ck),
        in_specs=[_rows(ck, RET_QK), _rows(ck, RET_QK), _rows(ck, RET_WIDTH), _rows(ck, RET_WIDTH),
                  tab, tab, _full((1, RET_WIDTH)), _full((1, RET_WIDTH)), st],
        out_specs=[_rows(ck, RET_WIDTH), st],
        out_shape=[jax.ShapeDtypeStruct((bn, L, RET_WIDTH), F32),
                   jax.ShapeDtypeStruct((bn, RET_HEADS, RET_DK, RET_DV), F32)],
        scratch_shapes=[pltpu.VMEM((RET_QK, RET_WIDTH), F32), pltpu.VMEM((RET_HEADS, ck, ck), F32),
                        pltpu.VMEM((RET_QK, RET_WIDTH), F32)],
        compiler_params=_cparams("parallel", "arbitrary"),
        name="ret_prompt",
    )(q, k, v, g, cosf, sinf, gn_g.reshape(1, -1), gn_b.reshape(1, -1), s0)


def _ret_sample_kernel(q_ref, k_ref, v_ref, g_ref, cos_ref, sin_ref, gng_ref, gnb_ref, s_ref, sbuf_ref,
                       o_ref, sout_ref, qt_scr, kt_scr, *, gb):
    del sbuf_ref
    i = pl.program_id(0)
    nb = q_ref.shape[0]

    @pl.when(i == 0)
    def _():
        cosf = jnp.broadcast_to(cos_ref[...], (nb, RET_QK))
        sinf = jnp.broadcast_to(sin_ref[...], (nb, RET_QK))
        qt_scr[...] = (_rope(q_ref[...], cosf, sinf) * (RET_DK ** -0.5)).T
        kt_scr[...] = _rope(k_ref[...], cosf, sinf).T

    lane = lax.broadcasted_iota(jnp.int32, (1, nb), 1)
    for j in range(gb):
        b = i * gb + j
        onehot = lane == b
        qcol = jnp.sum(jnp.where(onehot, qt_scr[...], 0.0), axis=1, keepdims=True)
        kcol = jnp.sum(jnp.where(onehot, kt_scr[...], 0.0), axis=1, keepdims=True)
        parts = []
        for h in range(RET_HEADS):
            s = s_ref[j, h]
            qc = qcol[h * RET_DK:(h + 1) * RET_DK]
            kc = kcol[h * RET_DK:(h + 1) * RET_DK]
            vrow = v_ref[j:j + 1, h * RET_DV:(h + 1) * RET_DV]
            gamma = math.exp(RET_LOG_DECAY[h])
            qk = jnp.sum(qc * kc, axis=0, keepdims=True)
            parts.append(qk * vrow + jnp.sum((qc * gamma) * s, axis=0, keepdims=True))
            sout_ref[j, h] = gamma * s + kc * vrow
        o = jnp.concatenate(parts, axis=1)
        o_ref[j:j + 1, :] = _group_norm_gate(o, g_ref[j:j + 1, :], gng_ref[...], gnb_ref[...])


def _ret_sample(q, k, v, g, cosf, sinf, s_all, layer, s_buf, gn_g, gn_b, gb=8):
    nb = q.shape[0]
    kern = functools.partial(_ret_sample_kernel, gb=gb)
    st = pl.BlockSpec((None, gb, RET_HEADS, RET_DK, RET_DV), lambda i: (layer, i, 0, 0, 0))
    return pl.pallas_call(
        kern,
        grid=(nb // gb,),
        in_specs=[_full((nb, RET_QK)), _full((nb, RET_QK)),
                  pl.BlockSpec((gb, RET_WIDTH), lambda i: (i, 0)), pl.BlockSpec((gb, RET_WIDTH), lambda i: (i, 0)),
                  _full((1, RET_QK)), _full((1, RET_QK)), _full((1, RET_WIDTH)), _full((1, RET_WIDTH)), st,
                  pl.BlockSpec(memory_space=pl.ANY)],
        out_specs=[pl.BlockSpec((gb, RET_WIDTH), lambda i: (i, 0)), st],
        out_shape=[jax.ShapeDtypeStruct((nb, RET_WIDTH), F32), jax.ShapeDtypeStruct(s_all.shape, F32)],
        input_output_aliases={9: 1},
        scratch_shapes=[pltpu.VMEM((RET_QK, nb), F32), pltpu.VMEM((RET_QK, nb), F32)],
        compiler_params=_cparams("arbitrary"),
        name="ret_sample",
    )(q, k, v, g, cosf, sinf, gn_g.reshape(1, -1), gn_b.reshape(1, -1), s_all, s_buf)


def _outproj_kernel(ya_ref, yb_ref, yc_ref, x_ref, w_ref, g_ref, b_ref, o_ref):
    mix = (_dot(ya_ref[...].astype(BF16), w_ref[0:S5_WIDTH, :])
           + _dot(yb_ref[...].astype(BF16), w_ref[S5_WIDTH:S5_WIDTH + SB_WIDTH, :])
           + _dot(yc_ref[...].astype(BF16), w_ref[S5_WIDTH + SB_WIDTH:, :]))
    o_ref[...] = _layer_norm(ALPHA * x_ref[...] + mix, g_ref[...], b_ref[...])


def _outproj(ya, yb, yc, x, w_all, layer, g, b, tl):
    bn, L, d = x.shape
    return pl.pallas_call(
        _outproj_kernel,
        grid=(bn, L // tl),
        in_specs=[_rows(tl, S5_WIDTH), _rows(tl, SB_WIDTH), _rows(tl, RET_WIDTH), _rows(tl, d),
                  pl.BlockSpec((None, d, d), lambda bi, t: (layer, 0, 0)), _full((1, d)), _full((1, d))],
        out_specs=_rows(tl, d),
        out_shape=jax.ShapeDtypeStruct((bn, L, d), F32),
        compiler_params=_cparams("parallel", "parallel"),
        name="outproj_ln",
    )(ya, yb, yc, x, w_all, g.reshape(1, d), b.reshape(1, d))


FFN_SUB = 256


def _conv_taps(h, hm1, hm2, cw_ref, cb_ref):
    return cb_ref[...] + (cw_ref[0:1, :] * hm2 + cw_ref[1:2, :] * hm1 + cw_ref[2:3, :] * h)


def _ffn_prompt_kernel(x_ref, wua_ref, wug_ref, cwa_ref, cwg_ref, cba_ref, cbg_ref, wd_ref, pa_ref, pg_ref,
                       g_ref, b_ref, o_ref, ca_ref, cg_ref, acc_scr, prev_scr, carry_scr, *, tl, nf, fc):
    t = pl.program_id(1)
    c = pl.program_id(2)

    @pl.when(c == 0)
    def _():
        acc_scr[...] = jnp.zeros_like(acc_scr)

    for part, prev_ref in ((0, pa_ref), (1, pg_ref)):
        @pl.when(t == 0)
        def _(part=part, prev_ref=prev_ref):
            prev_scr[part] = prev_ref[...]

        @pl.when(t > 0)
        def _(part=part):
            prev_scr[part] = carry_scr[c, part]

    x = x_ref[...]
    xb = x.astype(BF16)
    row8 = lax.broadcasted_iota(jnp.int32, (8, 1), 0)

    def conv(h, part, lo, hi, cw_ref, cb_ref):
        p0 = prev_scr[part, 0:1, lo:hi]
        p1 = prev_scr[part, 1:2, lo:hi]
        r1 = pltpu.roll(h, 1, 0)
        r2 = pltpu.roll(h, 2, 0)
        hm1 = jnp.concatenate([jnp.where(row8 == 0, p1, r1[0:8]), r1[8:]], axis=0)
        hm2 = jnp.concatenate([jnp.where(row8 == 0, p0, jnp.where(row8 == 1, p1, r2[0:8])), r2[8:]], axis=0)
        return cb_ref[:, lo:hi] + (cw_ref[0:1, lo:hi] * hm2 + cw_ref[1:2, lo:hi] * hm1 + cw_ref[2:3, lo:hi] * h)

    bounds = [(lo, min(lo + FFN_SUB, fc)) for lo in range(0, fc, FFN_SUB)]
    up = lambda k: (_dot(xb, wua_ref[:, bounds[k][0]:bounds[k][1]]), _dot(xb, wug_ref[:, bounds[k][0]:bounds[k][1]]))
    nxt = up(0)
    acc = None
    for k, (lo, hi) in enumerate(bounds):
        ha, hg = nxt
        if k + 1 < len(bounds):
            nxt = up(k + 1)
        ca = conv(ha, 0, lo, hi, cwa_ref, cba_ref)
        cg = conv(hg, 1, lo, hi, cwg_ref, cbg_ref)
        act = (cg * _sigmoid(cg) * ca).astype(BF16)
        down = _dot(act, wd_ref[lo:hi, :])
        acc = down if acc is None else acc + down
        for part, h, conv_ref in ((0, ha, ca_ref), (1, hg, cg_ref)):
            last2 = h[tl - 2:tl, :]
            carry_scr[c, part, :, lo:hi] = last2
            conv_ref[:, lo:hi] = last2
    acc_scr[...] += acc

    @pl.when(c == nf - 1)
    def _():
        o_ref[...] = _layer_norm(ALPHA * x + acc_scr[...], g_ref[...], b_ref[...])


def _ffn_prompt(x, prev, w_up_all, conv_w, conv_b, w_down_all, layer, g, b, tl, fc):
    bn, L, d = x.shape
    nf = D_FF // fc
    kern = functools.partial(_ffn_prompt_kernel, tl=tl, nf=nf, fc=fc)
    colblk = lambda rows, off: pl.BlockSpec((rows, fc), lambda bi, t, c: (0, c + off))
    wublk = lambda off: pl.BlockSpec((None, d, fc), lambda bi, t, c: (layer, 0, c + off))
    prevblk = lambda off: pl.BlockSpec((None, 2, fc), lambda bi, t, c: (bi, 0, c + off))
    xspec = pl.BlockSpec((None, tl, d), lambda bi, t, c: (bi, t, 0))
    vec = pl.BlockSpec((1, d), lambda bi, t, c: (0, 0))
    convout = pl.BlockSpec((None, None, 2, fc), lambda bi, t, c: (bi, t, 0, c))
    y, ca, cg = pl.pallas_call(
        kern,
        grid=(bn, L // tl, nf),
        in_specs=[xspec, wublk(0), wublk(nf), colblk(CONV_W, 0), colblk(CONV_W, nf),
                  colblk(1, 0), colblk(1, nf), pl.BlockSpec((None, fc, d), lambda bi, t, c: (layer, c, 0)),
                  prevblk(0), prevblk(nf), vec, vec],
        out_specs=[xspec, convout, convout],
        out_shape=[jax.ShapeDtypeStruct((bn, L, d), F32),
                   jax.ShapeDtypeStruct((bn, L // tl, 2, D_FF), F32),
                   jax.ShapeDtypeStruct((bn, L // tl, 2, D_FF), F32)],
        scratch_shapes=[pltpu.VMEM((tl, d), F32), pltpu.VMEM((2, 2, fc), F32),
                        pltpu.VMEM((nf, 2, 2, fc), F32)],
        compiler_params=_cparams("parallel", "arbitrary", "arbitrary"),
        name="ffn_prompt",
    )(x, w_up_all, w_up_all, conv_w, conv_w, conv_b, conv_b, w_down_all, prev, prev,
      g.reshape(1, d), b.reshape(1, d))
    return y, jnp.concatenate([ca[:, -1], cg[:, -1]], axis=-1)


def _ffn_sample_kernel(x_ref, wua_ref, wug_ref, cwa_ref, cwg_ref, cba_ref, cbg_ref, wd_ref,
                       p0a_ref, p0g_ref, p1a_ref, p1g_ref, g_ref, b_ref, o_ref, ha_ref, hg_ref, acc_scr, *, nf):
    c = pl.program_id(0)

    @pl.when(c == 0)
    def _():
        acc_scr[...] = jnp.zeros_like(acc_scr)

    x = x_ref[...]
    xb = x.astype(BF16)
    ha = _dot(xb, wua_ref[...])
    hg = _dot(xb, wug_ref[...])
    ha_ref[...] = ha
    hg_ref[...] = hg
    ca = _conv_taps(ha, p1a_ref[...], p0a_ref[...], cwa_ref, cba_ref)
    cg = _conv_taps(hg, p1g_ref[...], p0g_ref[...], cwg_ref, cbg_ref)
    act = (cg * _sigmoid(cg) * ca).astype(BF16)
    acc_scr[...] += _dot(act, wd_ref[...])

    @pl.when(c == nf - 1)
    def _():
        o_ref[...] = _layer_norm(ALPHA * x + acc_scr[...], g_ref[...], b_ref[...])


def _ffn_sample(x, prev, w_up_all, conv_w, conv_b, w_down_all, layer, g, b, fc):
    n, d = x.shape
    nf = D_FF // fc
    kern = functools.partial(_ffn_sample_kernel, nf=nf)
    colblk = lambda rows, off: pl.BlockSpec((rows, fc), lambda c: (0, c + off))
    wublk = lambda off: pl.BlockSpec((None, d, fc), lambda c: (layer, 0, c + off))
    prevblk = lambda j, off: pl.BlockSpec((n, fc), lambda c: (0, j * 2 * nf + c + off))
    prev2d = prev.reshape(n, (CONV_W - 1) * 2 * D_FF)
    full = lambda shape: pl.BlockSpec(shape, lambda c: (0, 0))
    hout = pl.BlockSpec((n, fc), lambda c: (0, c))
    y, ha, hg = pl.pallas_call(
        kern,
        grid=(nf,),
        in_specs=[full((n, d)), wublk(0), wublk(nf), colblk(CONV_W, 0), colblk(CONV_W, nf),
                  colblk(1, 0), colblk(1, nf), pl.BlockSpec((None, fc, d), lambda c: (layer, c, 0)),
                  prevblk(0, 0), prevblk(0, nf), prevblk(1, 0), prevblk(1, nf), full((1, d)), full((1, d))],
        out_specs=[full((n, d)), hout, hout],
        out_shape=[jax.ShapeDtypeStruct((n, d), F32),
                   jax.ShapeDtypeStruct((n, D_FF), F32), jax.ShapeDtypeStruct((n, D_FF), F32)],
        scratch_shapes=[pltpu.VMEM((n, d), F32)],
        compiler_params=_cparams("arbitrary"),
        name="ffn_sample",
    )(x, w_up_all, w_up_all, conv_w, conv_w, conv_b, conv_b, w_down_all, prev2d, prev2d, prev2d, prev2d,
      g.reshape(1, d), b.reshape(1, d))
    h_new = jnp.concatenate([ha, hg], axis=-1)
    return y, jnp.stack([prev[:, 1], h_new], axis=1)


def kernel(x_prompt, x_sample, cache_k, cache_v, state_s5_re, state_s5_im, state_ret, state_conv, page_table, ln_in_g, ln_in_b, w_in, s5_lambda_re, s5_lambda_im, s5_log_step, s5_b_re, s5_b_im, s5_c_re, s5_c_im, s5_d, s5_w_glu, norm_a_g, norm_b_g, sb_logit_bias, ret_gn_g, ret_gn_b, w_out, ln1_g, ln1_b, ffn_w_up, ffn_conv_w, ffn_conv_b, ffn_w_down, ln2_g, ln2_b):
    bp, lp, d = x_prompt.shape
    ns = x_sample.shape[0]
    past = page_table.shape[1] * PAGE_SIZE
    n_pool = cache_k.shape[1]
    ck_pages = cache_k.transpose(0, 1, 3, 4, 2).reshape(DEPTH, n_pool, SB_WIDTH, PAGE_SIZE)
    cv_pages = cache_v.transpose(0, 1, 3, 4, 2).reshape(DEPTH, n_pool, SB_WIDTH, PAGE_SIZE)

    def token_major(t):
        _, bn, _, L = t.shape
        return t.reshape(DEPTH, bn, SB_HEADS, SB_HEAD_DIM, L).transpose(0, 1, 4, 2, 3)

    w_in_b = w_in.astype(BF16)
    wkvt_b = w_in[:, :, KV_OFF:KV_OFF + 2 * SB_WIDTH].transpose(0, 2, 1).astype(BF16)
    w_out_b = w_out.astype(BF16)
    w_up_b = ffn_w_up.astype(BF16)
    w_down_b = ffn_w_down.astype(BF16)
    kt_p = jnp.zeros((DEPTH, bp, SB_WIDTH, lp), F32)
    vt_p = jnp.zeros((DEPTH, bp, SB_WIDTH, lp), F32)
    kt_s = jnp.zeros((DEPTH, 1, SB_WIDTH, ns), F32)
    vt_s = jnp.zeros((DEPTH, 1, SB_WIDTH, ns), F32)
    ret_s = jnp.zeros(state_ret.shape, F32)

    xp = _input_ln(x_prompt, ln_in_g, ln_in_b, 1024)
    xs = _input_ln(x_sample.reshape(1, ns, d), ln_in_g, ln_in_b, ns)
    cos_p, sin_p = _rope_tables(jnp.arange(lp, dtype=F32))
    cos_s, sin_s = _rope_tables(jnp.full((8,), past, F32))
    cos_s, sin_s = cos_s[0:1], sin_s[0:1]

    zero_s5 = jnp.zeros((bp, S5_FLAT), F32)
    zero_ret = jnp.zeros((bp, RET_HEADS, RET_DK, RET_DV), F32)
    zero_conv = jnp.zeros((bp, CONV_W - 1, 2 * D_FF), F32)
    outs = [[] for _ in range(8)]

    for l in range(DEPTH):
        w_glu_b = s5_w_glu[l].astype(BF16)
        cb = ffn_conv_b[l].reshape(1, -1)
        a_re, a_im, bd, cd = _s5_params(s5_lambda_re[l], s5_lambda_im[l], s5_log_step[l],
                                        s5_b_re[l], s5_b_im[l], s5_c_re[l], s5_c_im[l])

        u, qb, kt_p, vt_p, qc, kc, vc, gc = _inproj(xp, w_in_b, wkvt_b, l, kt_p, vt_p, 512)
        ya, s5r, s5i = _s5_branch(u.transpose(1, 0, 2).reshape(lp * bp, S5_WIDTH), zero_s5, zero_s5, a_re, a_im, bd, cd,
                                  s5_d[l], w_glu_b, norm_a_g[l], steps=128)
        yb = _sb_prompt(qb, kt_p, vt_p, l, sb_logit_bias[l], norm_b_g[l], 256)
        yc, ret_p = _ret_prompt(qc, kc, vc, gc, cos_p, sin_p, zero_ret, ret_gn_g[l], ret_gn_b[l], 256)
        x1 = _outproj(ya.reshape(lp, bp, S5_WIDTH).transpose(1, 0, 2), yb, yc, xp, w_out_b, l,
                      ln1_g[l], ln1_b[l], 512)
        xp, conv_p = _ffn_prompt(x1, zero_conv, w_up_b, ffn_conv_w[l], cb, w_down_b, l,
                                 ln2_g[l], ln2_b[l], 512, 1408)
        for i, a in enumerate((s5r.reshape(bp, S5_GROUPS, S5_STATE), s5i.reshape(bp, S5_GROUPS, S5_STATE),
                               ret_p, conv_p)):
            outs[i].append(a)

        u, qb, kt_s, vt_s, qc, kc, vc, gc = _inproj(xs, w_in_b, wkvt_b, l, kt_s, vt_s, ns)
        ya, s5r, s5i = _s5_branch(u.reshape(ns, S5_WIDTH), state_s5_re[l].reshape(ns, S5_FLAT),
                                  state_s5_im[l].reshape(ns, S5_FLAT), a_re, a_im, bd, cd,
                                  s5_d[l], w_glu_b, norm_a_g[l], steps=1)
        yb = _sb_sample(qb.reshape(ns, 1, SB_WIDTH), ck_pages, cv_pages, l, page_table,
                        sb_logit_bias[l], norm_b_g[l])
        yc, ret_s = _ret_sample(qc[0], kc[0], vc[0], gc[0], cos_s, sin_s, state_ret, l, ret_s,
                                ret_gn_g[l], ret_gn_b[l])
        x1 = _outproj(ya.reshape(1, ns, S5_WIDTH), yb.reshape(1, ns, SB_WIDTH), yc.reshape(1, ns, RET_WIDTH),
                      xs, w_out_b, l, ln1_g[l], ln1_b[l], ns)
        x2, conv_s = _ffn_sample(x1[0], state_conv[l], w_up_b, ffn_conv_w[l], cb, w_down_b, l,
                                 ln2_g[l], ln2_b[l], 1408)
        xs = x2.reshape(1, ns, d)
        for i, a in enumerate((s5r.reshape(ns, S5_GROUPS, S5_STATE), s5i.reshape(ns, S5_GROUPS, S5_STATE), conv_s)):
            outs[4 + i].append(a)

    stk = lambda i: jnp.stack(outs[i], axis=0)
    return (xp, xs.reshape(ns, 1, d),
            token_major(kt_p), token_major(vt_p),
            token_major(kt_s).reshape(DEPTH, ns, 1, SB_HEADS, SB_HEAD_DIM),
            token_major(vt_s).reshape(DEPTH, ns, 1, SB_HEADS, SB_HEAD_DIM),
            stk(0), stk(1), stk(4), stk(5),
            stk(2), ret_s,
            stk(3), stk(6))
```

```python
import functools
import math

import jax
import jax.numpy as jnp
from jax import lax
from jax.experimental import pallas as pl
from jax.experimental.pallas import tpu as pltpu

F32 = jnp.float32
BF16 = jnp.bfloat16

D_MODEL = 1024
DEPTH = 4
PAGE_SIZE = 128
S5_WIDTH = 256
S5_GROUP = 16
S5_GROUPS = 16
S5_STATE = 64
S5_FLAT = S5_GROUPS * S5_STATE
SB_HEADS = 4
SB_HEAD_DIM = 64
SB_WIDTH = 256
RET_WIDTH = 512
RET_HEADS = 4
RET_DV = 128
RET_DK = 64
RET_QK = 256
ROPE_BASE = 10000.0
D_FF = 2816
CONV_W = 3
LN_EPS = 1e-5
RMS_EPS = 1e-6
ALPHA = (2.0 * DEPTH) ** 0.25
SPLITS = (S5_WIDTH, SB_WIDTH, SB_WIDTH, SB_WIDTH, RET_QK, RET_QK, RET_WIDTH, RET_WIDTH)
D_IN = sum(SPLITS)
RET_LOG_DECAY = tuple(math.log1p(-(2.0 ** (-5.0 - h))) for h in range(RET_HEADS))

VMEM_LIMIT = 56 * 1024 * 1024


def _cparams(*sem):
    return pltpu.CompilerParams(dimension_semantics=sem, vmem_limit_bytes=VMEM_LIMIT)


def _dot(a, b):
    return jnp.dot(a, b, preferred_element_type=F32)


def _dot_nt(a, b):
    return lax.dot_general(a, b, (((1,), (1,)), ((), ())), preferred_element_type=F32)


def _dot_tn(a, b):
    return lax.dot_general(a, b, (((0,), (0,)), ((), ())), preferred_element_type=F32)


def _layer_norm(x, g, b):
    mu = jnp.mean(x, -1, keepdims=True)
    xc = x - mu
    var = jnp.mean(xc * xc, -1, keepdims=True)
    return xc * lax.rsqrt(var + LN_EPS) * g + b


def _rms_norm(x, g):
    return x * lax.rsqrt(jnp.mean(x * x, -1, keepdims=True) + RMS_EPS) * g


def _sigmoid(x):
    return 1.0 / (1.0 + jnp.exp(-x))


def _gelu_tanh(x):
    c = math.sqrt(2.0 / math.pi)
    return 0.5 * x * (1.0 + jnp.tanh(c * (x + 0.044715 * (x * x * x))))


def _full(shape):
    return pl.BlockSpec(shape, lambda *_: (0,) * len(shape))


def _rows(tl, w):
    return pl.BlockSpec((None, tl, w), lambda b, t: (b, t, 0))


def _ln_kernel(x_ref, g_ref, b_ref, o_ref):
    o_ref[...] = _layer_norm(x_ref[...], g_ref[...], b_ref[...])


def _input_ln(x, g, b, tl):
    bn, L, d = x.shape
    return pl.pallas_call(
        _ln_kernel,
        grid=(bn, L // tl),
        in_specs=[_rows(tl, d), _full((1, d)), _full((1, d))],
        out_specs=_rows(tl, d),
        out_shape=jax.ShapeDtypeStruct(x.shape, F32),
        compiler_params=_cparams("parallel", "parallel"),
        name="input_ln",
    )(x, g.reshape(1, d), b.reshape(1, d))


def _rope_table_kernel(pos_ref, inv_ref, cos_ref, sin_ref):
    ang = pos_ref[...] * inv_ref[...]
    lane = lax.broadcasted_iota(jnp.int32, ang.shape, 1)
    first = (lane % RET_DK) < (RET_DK // 2)
    cos_ref[...] = jnp.cos(ang)
    s = jnp.sin(ang)
    sin_ref[...] = jnp.where(first, -s, s)


def _rope_tables(pos):
    n = pos.shape[0]
    half = RET_DK // 2
    inv = ROPE_BASE ** (-jnp.arange(half, dtype=F32) / half)
    inv = jnp.tile(inv, 2 * RET_HEADS).reshape(1, RET_QK)
    return pl.pallas_call(
        _rope_table_kernel,
        out_shape=(jax.ShapeDtypeStruct((n, RET_QK), F32),) * 2,
        name="rope_tables",
    )(pos.reshape(n, 1), inv)


def _rope(x, cosf, sinf):
    lane = lax.broadcasted_iota(jnp.int32, (1, 128), 1)
    first = (lane % RET_DK) < (RET_DK // 2)
    halves = []
    for i in range(2):
        sl = slice(i * 128, (i + 1) * 128)
        xh = x[:, sl]
        partner = jnp.where(first, pltpu.roll(xh, 96, 1), pltpu.roll(xh, 32, 1))
        halves.append(xh * cosf[:, sl] + partner * sinf[:, sl])
    return jnp.concatenate(halves, axis=1)


def _s5_disc_kernel(lre_ref, lim_ref, ls_ref, bre_ref, bim_ref, are_ref, aim_ref, bbre_ref, bbim_ref):
    lam_re = jnp.minimum(lre_ref[...], -1e-4)
    lam_im = lim_ref[...]
    dt = jnp.exp(ls_ref[...])
    ldt_re, ldt_im = lam_re * dt, lam_im * dt
    mag = jnp.exp(ldt_re)
    a_re, a_im = mag * jnp.cos(ldt_im), mag * jnp.sin(ldt_im)
    den = lam_re * lam_re + lam_im * lam_im
    n_re, n_im = a_re - 1.0, a_im
    f_re = (n_re * lam_re + n_im * lam_im) / den
    f_im = (n_im * lam_re - n_re * lam_im) / den
    b_re, b_im = bre_ref[...], bim_ref[...]
    are_ref[...] = a_re
    aim_ref[...] = a_im
    bbre_ref[...] = f_re * b_re - f_im * b_im
    bbim_ref[...] = f_re * b_im + f_im * b_re


def _s5_params(lam_re, lam_im, log_step, b_re, b_im, c_re, c_im):
    G, P, H = S5_GROUPS, S5_STATE, S5_GROUP
    col = lambda a: a.reshape(G * P, 1)
    ls = jnp.broadcast_to(log_step[:, None], (G, P))
    a_re, a_im, bb_re, bb_im = pl.pallas_call(
        _s5_disc_kernel,
        out_shape=(jax.ShapeDtypeStruct((G * P, 1), F32),) * 2 + (jax.ShapeDtypeStruct((G * P, H), F32),) * 2,
        name="s5_discretize",
    )(col(lam_re), col(lam_im), col(ls), b_re.reshape(G * P, H), b_im.reshape(G * P, H))
    eye = jnp.eye(G, dtype=F32)

    def bdiag(bb):
        t = bb.reshape(G, P, H).transpose(0, 2, 1)
        return (t[:, :, None, :] * eye[:, None, :, None]).reshape(G * H, G * P)

    def cdiag(c):
        t = c.transpose(0, 2, 1)
        return (t[:, :, None, :] * eye[:, None, :, None]).reshape(G * P, G * H)

    bd = jnp.concatenate([bdiag(bb_re), bdiag(bb_im)], axis=1).astype(BF16)
    cd = jnp.concatenate([cdiag(c_re), -cdiag(c_im)], axis=0).astype(BF16)
    return a_re.reshape(1, G * P), a_im.reshape(1, G * P), bd, cd


def _inproj_kernel(x_ref, w_ref, *refs, n_alias):
    out_refs = refs[n_alias:]
    xb = x_ref[...].astype(BF16)
    off = 0
    for i, (ref, wd) in enumerate(zip(out_refs, SPLITS)):
        h = _dot(xb, w_ref[:, off:off + wd])
        ref[...] = h.T if i in (2, 3) else h
        off += wd


def _inproj(x, w_all, layer, kt_buf, vt_buf, tl):
    bn, L, d = x.shape
    alias = [] if kt_buf is None else [kt_buf, vt_buf]
    out_specs, out_shape = [], []
    for i, wd in enumerate(SPLITS):
        if i in (2, 3):
            out_specs.append(pl.BlockSpec((None, None, wd, tl), lambda b, t: (layer, b, 0, t)))
            out_shape.append(jax.ShapeDtypeStruct((DEPTH, bn, wd, L), F32))
        else:
            out_specs.append(_rows(tl, wd))
            out_shape.append(jax.ShapeDtypeStruct((bn, L, wd), F32))
    return pl.pallas_call(
        functools.partial(_inproj_kernel, n_alias=len(alias)),
        grid=(bn, L // tl),
        in_specs=[_rows(tl, d), pl.BlockSpec((None, d, D_IN), lambda b, t: (layer, 0, 0))]
        + [pl.BlockSpec(memory_space=pl.ANY)] * len(alias),
        out_specs=out_specs,
        out_shape=out_shape,
        input_output_aliases={2 + i: 2 + i for i in range(len(alias))},
        compiler_params=_cparams("parallel", "parallel"),
        name="inproj",
    )(x, w_all, *alias)


def _s5_kernel(u_ref, bd_ref, cd_ref, are_ref, aim_ref, d_ref, wglu_ref, g_ref, x0r_ref, x0i_ref,
               ya_ref, xr_out, xi_out, bu_scr, xs_scr, st_re, st_im, *, steps, bs):
    c = pl.program_id(0)

    @pl.when(c == 0)
    def _():
        st_re[...] = x0r_ref[...]
        st_im[...] = x0i_ref[...]

    u = u_ref[...]
    bu_scr[...] = _dot(u.astype(BF16), bd_ref[...])
    a_re = jnp.broadcast_to(are_ref[...], (bs, S5_FLAT))
    a_im = jnp.broadcast_to(aim_ref[...], (bs, S5_FLAT))

    def step(t, carry):
        xr, xi = carry
        r0 = pl.multiple_of(t * bs, bs)
        br = bu_scr[pl.ds(r0, bs), 0:S5_FLAT]
        bi = bu_scr[pl.ds(r0, bs), S5_FLAT:2 * S5_FLAT]
        nr = a_re * xr - a_im * xi + br
        ni = a_re * xi + a_im * xr + bi
        xs_scr[pl.ds(r0, bs), 0:S5_FLAT] = nr
        xs_scr[pl.ds(r0, bs), S5_FLAT:2 * S5_FLAT] = ni
        return nr, ni

    xr, xi = lax.fori_loop(0, steps, step, (st_re[...], st_im[...]))
    st_re[...] = xr
    st_im[...] = xi
    xr_out[...] = xr
    xi_out[...] = xi

    y = _dot(xs_scr[...].astype(BF16), cd_ref[...]) + d_ref[...] * u
    h = _gelu_tanh(y)
    out = h * _sigmoid(_dot(h.astype(BF16), wglu_ref[...]))
    ya_ref[...] = _rms_norm(out, g_ref[...])


def _s5_branch(u_tm, x0_re, x0_im, a_re, a_im, bd, cd, d, w_glu, g, steps):
    bs = x0_re.shape[0]
    rows = u_tm.shape[0]
    n_chunks = rows // (steps * bs)
    blk = steps * bs
    kern = functools.partial(_s5_kernel, steps=steps, bs=bs)
    vec = lambda n: _full((1, n))
    return pl.pallas_call(
        kern,
        grid=(n_chunks,),
        in_specs=[pl.BlockSpec((blk, S5_WIDTH), lambda c: (c, 0)),
                  _full(bd.shape), _full(cd.shape), vec(S5_FLAT), vec(S5_FLAT), vec(S5_WIDTH),
                  _full((S5_WIDTH, S5_WIDTH)), vec(S5_WIDTH), _full((bs, S5_FLAT)), _full((bs, S5_FLAT))],
        out_specs=[pl.BlockSpec((blk, S5_WIDTH), lambda c: (c, 0)), _full((bs, S5_FLAT)), _full((bs, S5_FLAT))],
        out_shape=[jax.ShapeDtypeStruct((rows, S5_WIDTH), F32),
                   jax.ShapeDtypeStruct((bs, S5_FLAT), F32), jax.ShapeDtypeStruct((bs, S5_FLAT), F32)],
        scratch_shapes=[pltpu.VMEM((blk, 2 * S5_FLAT), F32), pltpu.VMEM((blk, 2 * S5_FLAT), F32),
                        pltpu.VMEM((bs, S5_FLAT), F32), pltpu.VMEM((bs, S5_FLAT), F32)],
        compiler_params=_cparams("arbitrary"),
        name="s5_branch",
    )(u_tm, bd, cd, a_re, a_im, d.reshape(1, -1), w_glu, g.reshape(1, -1), x0_re, x0_im)


def _sb_logits(z):
    t = jnp.log(1.0 + jnp.exp(-jnp.abs(z)))
    lb = jnp.minimum(z, 0.0) - t
    return lb, lb - z


def _sb_prompt_kernel(bias_ref, q_ref, k_ref, v_ref, g_ref, o_ref, kb_scr, vb_scr, acc_scr, *, tq):
    qi = pl.program_id(1)

    @pl.when(qi == 0)
    def _():
        for jj in range(kb_scr.shape[0]):
            kb_scr[jj] = k_ref[:, jj * tq:(jj + 1) * tq].astype(BF16)
            vb_scr[jj] = v_ref[:, jj * tq:(jj + 1) * tq].astype(BF16)

    rows = SB_HEADS * tq
    q = q_ref[...] * (SB_HEAD_DIM ** -0.5)
    lane = lax.broadcasted_iota(jnp.int32, (1, SB_WIDTH), 1)
    q4 = jnp.concatenate([jnp.where(lane // SB_HEAD_DIM == h, q, 0.0) for h in range(SB_HEADS)],
                         axis=0).astype(BF16)
    rblock = lax.broadcasted_iota(jnp.int32, (rows, 1), 0) // tq
    bias = jnp.zeros((rows, 1), F32)
    for h in range(SB_HEADS):
        bias = jnp.where(rblock == h, bias_ref[h], bias)
    qpos = lax.broadcasted_iota(jnp.int32, (rows, tq), 0) % tq
    kpos = lax.broadcasted_iota(jnp.int32, (rows, tq), 1)
    causal = kpos < qpos
    later = (lax.broadcasted_iota(jnp.int32, (tq, tq), 0)
             > lax.broadcasted_iota(jnp.int32, (tq, tq), 1)).astype(BF16)

    def tile(j, carry, masked):
        z = _dot(q4, kb_scr[j]) + bias
        lb, lr = _sb_logits(z)
        if masked:
            lr = jnp.where(causal, lr, 0.0)
        tail = _dot(lr.astype(BF16), later) + carry
        w = jnp.exp(lb + tail)
        if masked:
            w = jnp.where(causal, w, 0.0)
        acc_scr[...] += _dot_nt(w.astype(BF16), vb_scr[j])
        return carry + jnp.sum(lr, axis=1, keepdims=True)

    def tile_pair(i, carry):
        ja = qi - 1 - 2 * i
        jb = ja - 1
        lb_a, lr_a = _sb_logits(_dot(q4, kb_scr[ja]) + bias)
        lb_b, lr_b = _sb_logits(_dot(q4, kb_scr[jb]) + bias)
        carry_b = carry + jnp.sum(lr_a, axis=1, keepdims=True)
        w_a = jnp.exp(lb_a + (_dot(lr_a.astype(BF16), later) + carry))
        w_b = jnp.exp(lb_b + (_dot(lr_b.astype(BF16), later) + carry_b))
        acc_scr[...] += _dot_nt(w_a.astype(BF16), vb_scr[ja]) + _dot_nt(w_b.astype(BF16), vb_scr[jb])
        return carry_b + jnp.sum(lr_b, axis=1, keepdims=True)

    acc_scr[...] = jnp.zeros_like(acc_scr)
    carry = tile(qi, jnp.zeros((rows, 1), F32), True)
    carry = lax.fori_loop(0, lax.shift_right_logical(qi, 1), tile_pair, carry)

    @pl.when((qi & 1) == 1)
    def _():
        tile(0, carry, False)

    out = jnp.zeros((tq, SB_WIDTH), F32)
    for h in range(SB_HEADS):
        out = jnp.where(lane // SB_HEAD_DIM == h, acc_scr[h * tq:(h + 1) * tq, :], out)
    o_ref[...] = _rms_norm(out, g_ref[...])


def _sb_prompt(q, kt, vt, layer, bias, g, tq):
    bn, L, w = q.shape
    kern = functools.partial(_sb_prompt_kernel, tq=tq)
    seq = pl.BlockSpec((None, None, w, L), lambda b, t: (layer, b, 0, 0))
    return pl.pallas_call(
        kern,
        grid=(bn, L // tq),
        in_specs=[pl.BlockSpec(memory_space=pltpu.SMEM), _rows(tq, w), seq, seq, _full((1, w))],
        out_specs=_rows(tq, w),
        out_shape=jax.ShapeDtypeStruct((bn, L, w), F32),
        scratch_shapes=[pltpu.VMEM((L // tq, w, tq), BF16), pltpu.VMEM((L // tq, w, tq), BF16),
                        pltpu.VMEM((SB_HEADS * tq, w), F32)],
        compiler_params=_cparams("parallel", "arbitrary"),
        name="sb_prompt",
    )(bias, q, kt, vt, g.reshape(1, w))


def _sb_sample_kernel(pt_ref, bias_ref, q_ref, *refs, n_pages):
    del pt_ref
    k_refs, v_refs = refs[:n_pages], refs[n_pages:2 * n_pages]
    g_ref, o_ref = refs[2 * n_pages], refs[2 * n_pages + 1]
    sub = lax.broadcasted_iota(jnp.int32, (8, SB_WIDTH), 0)
    lane = lax.broadcasted_iota(jnp.int32, (8, SB_WIDTH), 1)
    diag = (lane // SB_HEAD_DIM) == sub
    q = q_ref[...] * (SB_HEAD_DIM ** -0.5)
    qblk = jnp.where(diag, jnp.broadcast_to(q, (8, SB_WIDTH)), 0.0).astype(BF16)
    sub1 = lax.broadcasted_iota(jnp.int32, (8, 1), 0)
    bias = jnp.zeros((8, 1), F32)
    for h in range(SB_HEADS):
        bias = jnp.where(sub1 == h, bias_ref[h], bias)

    lbs, lrs = [], []
    for p in range(n_pages):
        z = _dot(qblk, k_refs[p][...].astype(BF16)) + bias
        lb, lr = _sb_logits(z)
        lbs.append(lb)
        lrs.append(lr)

    r = lax.broadcasted_iota(jnp.int32, (PAGE_SIZE, PAGE_SIZE), 0)
    c = lax.broadcasted_iota(jnp.int32, (PAGE_SIZE, PAGE_SIZE), 1)
    later = (r > c).astype(BF16)
    lr_all = jnp.concatenate(lrs, axis=0)
    hi = lr_all.astype(BF16)
    lo = (lr_all - hi.astype(F32)).astype(BF16)
    tail_in_page = _dot(hi, later) + _dot(lo, later)

    carry = jnp.zeros((8, 1), F32)
    acc = jnp.zeros((8, SB_WIDTH), F32)
    for p in reversed(range(n_pages)):
        w = jnp.exp(lbs[p] + tail_in_page[8 * p:8 * p + 8] + carry)
        acc = acc + _dot_nt(w.astype(BF16), v_refs[p][...].astype(BF16))
        carry = carry + jnp.sum(lrs[p], axis=1, keepdims=True)
    out = jnp.sum(jnp.where(diag, acc, 0.0), axis=0, keepdims=True)
    o_ref[...] = _rms_norm(out, g_ref[...])


def _sb_sample(q, cache_k, cache_v, layer, page_table, bias, g):
    bn = q.shape[0]
    n_pages = page_table.shape[1]
    kern = functools.partial(_sb_sample_kernel, n_pages=n_pages)

    def page_spec(p):
        return pl.BlockSpec((None, None, SB_WIDTH, PAGE_SIZE), lambda b, pt: (layer, pt[b, p], 0, 0))

    row = pl.BlockSpec((None, 1, SB_WIDTH), lambda b, pt: (b, 0, 0))
    grid_spec = pltpu.PrefetchScalarGridSpec(
        num_scalar_prefetch=1,
        grid=(bn,),
        in_specs=[pl.BlockSpec(memory_space=pltpu.SMEM), row]
        + [page_spec(p) for p in range(n_pages)] * 2
        + [pl.BlockSpec((1, SB_WIDTH), lambda b, pt: (0, 0))],
        out_specs=row,
    )
    return pl.pallas_call(
        kern,
        grid_spec=grid_spec,
        out_shape=jax.ShapeDtypeStruct((bn, 1, SB_WIDTH), F32),
        compiler_params=_cparams("arbitrary"),
        name="sb_sample",
    )(page_table, bias, q, *([cache_k] * n_pages), *([cache_v] * n_pages), g.reshape(1, -1))


def _head_lane_consts(width, per_head):
    lane = lax.broadcasted_iota(jnp.int32, (1, width), 1)
    lg = jnp.zeros((1, width), F32)
    for h in range(RET_HEADS):
        lg = jnp.where(lane // per_head == h, RET_LOG_DECAY[h], lg)
    return lg


def _group_norm_gate(o, g, gn_g, gn_b):
    parts = []
    for h in range(RET_HEADS):
        oh = o[:, h * RET_DV:(h + 1) * RET_DV]
        mu = jnp.mean(oh, -1, keepdims=True)
        oc = oh - mu
        var = jnp.mean(oc * oc, -1, keepdims=True)
        parts.append(oc * lax.rsqrt(var + LN_EPS))
    on = jnp.concatenate(parts, axis=1) * gn_g + gn_b
    return g * _sigmoid(g) * on


def _ret_prompt_kernel(q_ref, k_ref, v_ref, g_ref, cos_ref, sin_ref, gng_ref, gnb_ref, s0_ref,
                       o_ref, sout_ref, s_scr, dec_scr, gam_scr, *, ck):
    c = pl.program_id(1)
    rowi = lax.broadcasted_iota(jnp.int32, (RET_QK, RET_WIDTH), 0)
    coli = lax.broadcasted_iota(jnp.int32, (RET_QK, RET_WIDTH), 1)
    blockmask = (rowi // RET_DK) == (coli // RET_DV)

    @pl.when(c == 0)
    def _():
        s_scr[...] = jnp.zeros_like(s_scr)
        di = lax.broadcasted_iota(jnp.int32, (ck, ck), 0) - lax.broadcasted_iota(jnp.int32, (ck, ck), 1)
        df = jnp.maximum(di, 0).astype(F32)
        gam = jnp.zeros((RET_QK, RET_WIDTH), F32)
        for h in range(RET_HEADS):
            s_scr[h * RET_DK:(h + 1) * RET_DK, h * RET_DV:(h + 1) * RET_DV] = s0_ref[h]
            dec_scr[h] = jnp.where(di >= 0, jnp.exp(RET_LOG_DECAY[h] * df), 0.0)
            gam = jnp.where((rowi // RET_DK == h) & blockmask, math.exp(RET_LOG_DECAY[h] * ck), gam)
        gam_scr[...] = gam

    cosf, sinf = cos_ref[...], sin_ref[...]
    q = _rope(q_ref[...], cosf, sinf) * (RET_DK ** -0.5)
    k = _rope(k_ref[...], cosf, sinf)
    v = v_ref[...]
    kb, vb = k.astype(BF16), v.astype(BF16)
    lane = lax.broadcasted_iota(jnp.int32, (1, RET_QK), 1)
    idx = lax.broadcasted_iota(jnp.int32, (ck, 1), 0).astype(F32)
    lg = _head_lane_consts(RET_QK, RET_DK)

    intra = []
    for h in range(RET_HEADS):
        hm = (lane // RET_DK) == h
        inner = _dot_nt(jnp.where(hm, q, 0.0).astype(BF16), kb)
        p = (inner * dec_scr[h]).astype(BF16)
        intra.append(_dot(p, vb[:, h * RET_DV:(h + 1) * RET_DV]))
    q_dec = q * jnp.exp(lg * (idx + 1.0))
    o = jnp.concatenate(intra, axis=1) + _dot(q_dec.astype(BF16), s_scr[...].astype(BF16))
    k_dec = k * jnp.exp(lg * (ck - 1.0 - idx))
    kv = _dot_tn(k_dec.astype(BF16), vb)
    s_new = gam_scr[...] * s_scr[...] + jnp.where(blockmask, kv, 0.0)
    s_scr[...] = s_new
    for h in range(RET_HEADS):
        sout_ref[h] = s_new[h * RET_DK:(h + 1) * RET_DK, h * RET_DV:(h + 1) * RET_DV]
    o_ref[...] = _group_norm_gate(o, g_ref[...], gng_ref[...], gnb_ref[...])


def _ret_prompt(q, k, v, g, cosf, sinf, s0, gn_g, gn_b, ck):
    bn, L, _ = q.shape
    kern = functools.partial(_ret_prompt_kernel, ck=ck)
    tab = pl.BlockSpec((ck, RET_QK), lambda b, c: (c, 0))
    st = pl.BlockSpec((None, RET_HEADS, RET_DK, RET_DV), lambda b, c: (b, 0, 0, 0))
    return pl.pallas_call(
        kern,
        grid=(bn, L // ck),
        in_specs=[_rows(ck, RET_QK), _rows(ck, RET_QK), _rows(ck, RET_WIDTH), _rows(ck, RET_WIDTH),
                  tab, tab, _full((1, RET_WIDTH)), _full((1, RET_WIDTH)), st],
        out_specs=[_rows(ck, RET_WIDTH), st],
        out_shape=[jax.ShapeDtypeStruct((bn, L, RET_WIDTH), F32),
                   jax.ShapeDtypeStruct((bn, RET_HEADS, RET_DK, RET_DV), F32)],
        scratch_shapes=[pltpu.VMEM((RET_QK, RET_WIDTH), F32), pltpu.VMEM((RET_HEADS, ck, ck), F32),
                        pltpu.VMEM((RET_QK, RET_WIDTH), F32)],
        compiler_params=_cparams("parallel", "arbitrary"),
        name="ret_prompt",
    )(q, k, v, g, cosf, sinf, gn_g.reshape(1, -1), gn_b.reshape(1, -1), s0)


def _ret_sample_kernel(q_ref, k_ref, v_ref, g_ref, cos_ref, sin_ref, gng_ref, gnb_ref, s_ref, *refs, gb):
    o_ref, sout_ref, qt_scr, kt_scr = refs[-4:]
    i = pl.program_id(0)
    nb = q_ref.shape[0]

    @pl.when(i == 0)
    def _():
        cosf = jnp.broadcast_to(cos_ref[...], (nb, RET_QK))
        sinf = jnp.broadcast_to(sin_ref[...], (nb, RET_QK))
        qt_scr[...] = (_rope(q_ref[...], cosf, sinf) * (RET_DK ** -0.5)).T
        kt_scr[...] = _rope(k_ref[...], cosf, sinf).T

    lane = lax.broadcasted_iota(jnp.int32, (1, nb), 1)
    for j in range(gb):
        b = i * gb + j
        onehot = lane == b
        qcol = jnp.sum(jnp.where(onehot, qt_scr[...], 0.0), axis=1, keepdims=True)
        kcol = jnp.sum(jnp.where(onehot, kt_scr[...], 0.0), axis=1, keepdims=True)
        parts = []
        for h in range(RET_HEADS):
            s = s_ref[j, h]
            qc = qcol[h * RET_DK:(h + 1) * RET_DK]
            kc = kcol[h * RET_DK:(h + 1) * RET_DK]
            vrow = v_ref[j:j + 1, h * RET_DV:(h + 1) * RET_DV]
            gamma = math.exp(RET_LOG_DECAY[h])
            qk = jnp.sum(qc * kc, axis=0, keepdims=True)
            parts.append(qk * vrow + jnp.sum((qc * gamma) * s, axis=0, keepdims=True))
            sout_ref[j, h] = gamma * s + kc * vrow
        o = jnp.concatenate(parts, axis=1)
        o_ref[j:j + 1, :] = _group_norm_gate(o, g_ref[j:j + 1, :], gng_ref[...], gnb_ref[...])


def _ret_sample(q, k, v, g, cosf, sinf, s_all, layer, s_buf, gn_g, gn_b, gb=8):
    nb = q.shape[0]
    alias = [] if s_buf is None else [s_buf]
    kern = functools.partial(_ret_sample_kernel, gb=gb)
    st = pl.BlockSpec((None, gb, RET_HEADS, RET_DK, RET_DV), lambda i: (layer, i, 0, 0, 0))
    return pl.pallas_call(
        kern,
        grid=(nb // gb,),
        in_specs=[_full((nb, RET_QK)), _full((nb, RET_QK)),
                  pl.BlockSpec((gb, RET_WIDTH), lambda i: (i, 0)), pl.BlockSpec((gb, RET_WIDTH), lambda i: (i, 0)),
                  _full((1, RET_QK)), _full((1, RET_QK)), _full((1, RET_WIDTH)), _full((1, RET_WIDTH)), st]
        + [pl.BlockSpec(memory_space=pl.ANY)] * len(alias),
        out_specs=[pl.BlockSpec((gb, RET_WIDTH), lambda i: (i, 0)), st],
        out_shape=[jax.ShapeDtypeStruct((nb, RET_WIDTH), F32), jax.ShapeDtypeStruct(s_all.shape, F32)],
        input_output_aliases={9: 1} if alias else {},
        scratch_shapes=[pltpu.VMEM((RET_QK, nb), F32), pltpu.VMEM((RET_QK, nb), F32)],
        compiler_params=_cparams("arbitrary"),
        name="ret_sample",
    )(q, k, v, g, cosf, sinf, gn_g.reshape(1, -1), gn_b.reshape(1, -1), s_all, *alias)


def _outproj_kernel(ya_ref, yb_ref, yc_ref, x_ref, w_ref, g_ref, b_ref, o_ref):
    mix = (_dot(ya_ref[...].astype(BF16), w_ref[0:S5_WIDTH, :])
           + _dot(yb_ref[...].astype(BF16), w_ref[S5_WIDTH:S5_WIDTH + SB_WIDTH, :])
           + _dot(yc_ref[...].astype(BF16), w_ref[S5_WIDTH + SB_WIDTH:, :]))
    o_ref[...] = _layer_norm(ALPHA * x_ref[...] + mix, g_ref[...], b_ref[...])


def _outproj(ya, yb, yc, x, w_all, layer, g, b, tl):
    bn, L, d = x.shape
    return pl.pallas_call(
        _outproj_kernel,
        grid=(bn, L // tl),
        in_specs=[_rows(tl, S5_WIDTH), _rows(tl, SB_WIDTH), _rows(tl, RET_WIDTH), _rows(tl, d),
                  pl.BlockSpec((None, d, d), lambda bi, t: (layer, 0, 0)), _full((1, d)), _full((1, d))],
        out_specs=_rows(tl, d),
        out_shape=jax.ShapeDtypeStruct((bn, L, d), F32),
        compiler_params=_cparams("parallel", "parallel"),
        name="outproj_ln",
    )(ya, yb, yc, x, w_all, g.reshape(1, d), b.reshape(1, d))


FFN_SUB = 256


def _conv_taps(h, hm1, hm2, cw_ref, cb_ref):
    return cb_ref[...] + (cw_ref[0:1, :] * hm2 + cw_ref[1:2, :] * hm1 + cw_ref[2:3, :] * h)


def _ffn_prompt_kernel(x_ref, wua_ref, wug_ref, cwa_ref, cwg_ref, cba_ref, cbg_ref, wd_ref, pa_ref, pg_ref,
                       g_ref, b_ref, o_ref, ca_ref, cg_ref, acc_scr, prev_scr, carry_scr, *, tl, nf, fc):
    t = pl.program_id(1)
    c = pl.program_id(2)

    @pl.when(c == 0)
    def _():
        acc_scr[...] = jnp.zeros_like(acc_scr)

    for part, prev_ref in ((0, pa_ref), (1, pg_ref)):
        @pl.when(t == 0)
        def _(part=part, prev_ref=prev_ref):
            prev_scr[part] = prev_ref[...]

        @pl.when(t > 0)
        def _(part=part):
            prev_scr[part] = carry_scr[c, part]

    x = x_ref[...]
    xb = x.astype(BF16)
    row8 = lax.broadcasted_iota(jnp.int32, (8, 1), 0)

    def conv(h, part, lo, hi, cw_ref, cb_ref):
        p0 = prev_scr[part, 0:1, lo:hi]
        p1 = prev_scr[part, 1:2, lo:hi]
        r1 = pltpu.roll(h, 1, 0)
        r2 = pltpu.roll(h, 2, 0)
        hm1 = jnp.concatenate([jnp.where(row8 == 0, p1, r1[0:8]), r1[8:]], axis=0)
        hm2 = jnp.concatenate([jnp.where(row8 == 0, p0, jnp.where(row8 == 1, p1, r2[0:8])), r2[8:]], axis=0)
        return cb_ref[:, lo:hi] + (cw_ref[0:1, lo:hi] * hm2 + cw_ref[1:2, lo:hi] * hm1 + cw_ref[2:3, lo:hi] * h)

    bounds = [(lo, min(lo + FFN_SUB, fc)) for lo in range(0, fc, FFN_SUB)]
    up = lambda k: (_dot(xb, wua_ref[:, bounds[k][0]:bounds[k][1]]), _dot(xb, wug_ref[:, bounds[k][0]:bounds[k][1]]))
    nxt = up(0)
    acc = None
    for k, (lo, hi) in enumerate(bounds):
        ha, hg = nxt
        if k + 1 < len(bounds):
            nxt = up(k + 1)
        ca = conv(ha, 0, lo, hi, cwa_ref, cba_ref)
        cg = conv(hg, 1, lo, hi, cwg_ref, cbg_ref)
        act = (cg * _sigmoid(cg) * ca).astype(BF16)
        down = _dot(act, wd_ref[lo:hi, :])
        acc = down if acc is None else acc + down
        for part, h, conv_ref in ((0, ha, ca_ref), (1, hg, cg_ref)):
            last2 = h[tl - 2:tl, :]
            carry_scr[c, part, :, lo:hi] = last2
            conv_ref[:, lo:hi] = last2
    acc_scr[...] += acc

    @pl.when(c == nf - 1)
    def _():
        o_ref[...] = _layer_norm(ALPHA * x + acc_scr[...], g_ref[...], b_ref[...])


def _ffn_prompt(x, prev, w_up_all, conv_w, conv_b, w_down_all, layer, g, b, tl, fc):
    bn, L, d = x.shape
    nf = D_FF // fc
    kern = functools.partial(_ffn_prompt_kernel, tl=tl, nf=nf, fc=fc)
    colblk = lambda rows, off: pl.BlockSpec((rows, fc), lambda bi, t, c: (0, c + off))
    wublk = lambda off: pl.BlockSpec((None, d, fc), lambda bi, t, c: (layer, 0, c + off))
    prevblk = lambda off: pl.BlockSpec((None, 2, fc), lambda bi, t, c: (bi, 0, c + off))
    xspec = pl.BlockSpec((None, tl, d), lambda bi, t, c: (bi, t, 0))
    vec = pl.BlockSpec((1, d), lambda bi, t, c: (0, 0))
    convout = pl.BlockSpec((None, None, 2, fc), lambda bi, t, c: (bi, t, 0, c))
    y, ca, cg = pl.pallas_call(
        kern,
        grid=(bn, L // tl, nf),
        in_specs=[xspec, wublk(0), wublk(nf), colblk(CONV_W, 0), colblk(CONV_W, nf),
                  colblk(1, 0), colblk(1, nf), pl.BlockSpec((None, fc, d), lambda bi, t, c: (layer, c, 0)),
                  prevblk(0), prevblk(nf), vec, vec],
        out_specs=[xspec, convout, convout],
        out_shape=[jax.ShapeDtypeStruct((bn, L, d), F32),
                   jax.ShapeDtypeStruct((bn, L // tl, 2, D_FF), F32),
                   jax.ShapeDtypeStruct((bn, L // tl, 2, D_FF), F32)],
        scratch_shapes=[pltpu.VMEM((tl, d), F32), pltpu.VMEM((2, 2, fc), F32),
                        pltpu.VMEM((nf, 2, 2, fc), F32)],
        compiler_params=_cparams("parallel", "arbitrary", "arbitrary"),
        name="ffn_prompt",
    )(x, w_up_all, w_up_all, conv_w, conv_w, conv_b, conv_b, w_down_all, prev, prev,
      g.reshape(1, d), b.reshape(1, d))
    return y, jnp.concatenate([ca[:, -1], cg[:, -1]], axis=-1)


def _ffn_sample_kernel(x_ref, wua_ref, wug_ref, cwa_ref, cwg_ref, cba_ref, cbg_ref, wd_ref,
                       p0a_ref, p0g_ref, p1a_ref, p1g_ref, g_ref, b_ref, o_ref, ha_ref, hg_ref, acc_scr, *, nf):
    c = pl.program_id(0)

    @pl.when(c == 0)
    def _():
        acc_scr[...] = jnp.zeros_like(acc_scr)

    x = x_ref[...]
    xb = x.astype(BF16)
    ha = _dot(xb, wua_ref[...])
    hg = _dot(xb, wug_ref[...])
    ha_ref[...] = ha
    hg_ref[...] = hg
    ca = _conv_taps(ha, p1a_ref[...], p0a_ref[...], cwa_ref, cba_ref)
    cg = _conv_taps(hg, p1g_ref[...], p0g_ref[...], cwg_ref, cbg_ref)
    act = (cg * _sigmoid(cg) * ca).astype(BF16)
    acc_scr[...] += _dot(act, wd_ref[...])

    @pl.when(c == nf - 1)
    def _():
        o_ref[...] = _layer_norm(ALPHA * x + acc_scr[...], g_ref[...], b_ref[...])


def _ffn_sample(x, prev, w_up_all, conv_w, conv_b, w_down_all, layer, g, b, fc):
    n, d = x.shape
    nf = D_FF // fc
    kern = functools.partial(_ffn_sample_kernel, nf=nf)
    colblk = lambda rows, off: pl.BlockSpec((rows, fc), lambda c: (0, c + off))
    wublk = lambda off: pl.BlockSpec((None, d, fc), lambda c: (layer, 0, c + off))
    prevblk = lambda j, off: pl.BlockSpec((n, fc), lambda c: (0, j * 2 * nf + c + off))
    prev2d = prev.reshape(n, (CONV_W - 1) * 2 * D_FF)
    full = lambda shape: pl.BlockSpec(shape, lambda c: (0, 0))
    hout = pl.BlockSpec((n, fc), lambda c: (0, c))
    y, ha, hg = pl.pallas_call(
        kern,
        grid=(nf,),
        in_specs=[full((n, d)), wublk(0), wublk(nf), colblk(CONV_W, 0), colblk(CONV_W, nf),
                  colblk(1, 0), colblk(1, nf), pl.BlockSpec((None, fc, d), lambda c: (layer, c, 0)),
                  prevblk(0, 0), prevblk(0, nf), prevblk(1, 0), prevblk(1, nf), full((1, d)), full((1, d))],
        out_specs=[full((n, d)), hout, hout],
        out_shape=[jax.ShapeDtypeStruct((n, d), F32),
                   jax.ShapeDtypeStruct((n, D_FF), F32), jax.ShapeDtypeStruct((n, D_FF), F32)],
        scratch_shapes=[pltpu.VMEM((n, d), F32)],
        compiler_params=_cparams("arbitrary"),
        name="ffn_sample",
    )(x, w_up_all, w_up_all, conv_w, conv_w, conv_b, conv_b, w_down_all, prev2d, prev2d, prev2d, prev2d,
      g.reshape(1, d), b.reshape(1, d))
    h_new = jnp.concatenate([ha, hg], axis=-1)
    return y, jnp.stack([prev[:, 1], h_new], axis=1)


def kernel(x_prompt, x_sample, cache_k, cache_v, state_s5_re, state_s5_im, state_ret, state_conv, page_table, ln_in_g, ln_in_b, w_in, s5_lambda_re, s5_lambda_im, s5_log_step, s5_b_re, s5_b_im, s5_c_re, s5_c_im, s5_d, s5_w_glu, norm_a_g, norm_b_g, sb_logit_bias, ret_gn_g, ret_gn_b, w_out, ln1_g, ln1_b, ffn_w_up, ffn_conv_w, ffn_conv_b, ffn_w_down, ln2_g, ln2_b):
    bp, lp, d = x_prompt.shape
    ns = x_sample.shape[0]
    past = page_table.shape[1] * PAGE_SIZE
    n_pool = cache_k.shape[1]
    ck_pages = cache_k.transpose(0, 1, 3, 4, 2).reshape(DEPTH, n_pool, SB_WIDTH, PAGE_SIZE)
    cv_pages = cache_v.transpose(0, 1, 3, 4, 2).reshape(DEPTH, n_pool, SB_WIDTH, PAGE_SIZE)

    def token_major(t):
        _, bn, _, L = t.shape
        return t.reshape(DEPTH, bn, SB_HEADS, SB_HEAD_DIM, L).transpose(0, 1, 4, 2, 3)

    w_in_b = w_in.astype(BF16)
    w_out_b = w_out.astype(BF16)
    w_up_b = ffn_w_up.astype(BF16)
    w_down_b = ffn_w_down.astype(BF16)
    kt_p = vt_p = kt_s = vt_s = ret_s = None

    xp = _input_ln(x_prompt, ln_in_g, ln_in_b, 1024)
    xs = _input_ln(x_sample.reshape(1, ns, d), ln_in_g, ln_in_b, ns)
    cos_p, sin_p = _rope_tables(jnp.arange(lp, dtype=F32))
    cos_s, sin_s = _rope_tables(jnp.full((8,), past, F32))
    cos_s, sin_s = cos_s[0:1], sin_s[0:1]

    zero_s5 = jnp.zeros((bp, S5_FLAT), F32)
    zero_ret = jnp.zeros((bp, RET_HEADS, RET_DK, RET_DV), F32)
    zero_conv = jnp.zeros((bp, CONV_W - 1, 2 * D_FF), F32)
    outs = [[] for _ in range(8)]

    for l in range(DEPTH):
        w_glu_b = s5_w_glu[l].astype(BF16)
        cb = ffn_conv_b[l].reshape(1, -1)
        a_re, a_im, bd, cd = _s5_params(s5_lambda_re[l], s5_lambda_im[l], s5_log_step[l],
                                        s5_b_re[l], s5_b_im[l], s5_c_re[l], s5_c_im[l])

        u, qb, kt_p, vt_p, qc, kc, vc, gc = _inproj(xp, w_in_b, l, kt_p, vt_p, 512)
        ya, s5r, s5i = _s5_branch(u.transpose(1, 0, 2).reshape(lp * bp, S5_WIDTH), zero_s5, zero_s5, a_re, a_im, bd, cd,
                                  s5_d[l], w_glu_b, norm_a_g[l], steps=128)
        yb = _sb_prompt(qb, kt_p, vt_p, l, sb_logit_bias[l], norm_b_g[l], 256)
        yc, ret_p = _ret_prompt(qc, kc, vc, gc, cos_p, sin_p, zero_ret, ret_gn_g[l], ret_gn_b[l], 256)
        x1 = _outproj(ya.reshape(lp, bp, S5_WIDTH).transpose(1, 0, 2), yb, yc, xp, w_out_b, l,
                      ln1_g[l], ln1_b[l], 1024)
        xp, conv_p = _ffn_prompt(x1, zero_conv, w_up_b, ffn_conv_w[l], cb, w_down_b, l,
                                 ln2_g[l], ln2_b[l], 512, 1408)
        for i, a in enumerate((s5r.reshape(bp, S5_GROUPS, S5_STATE), s5i.reshape(bp, S5_GROUPS, S5_STATE),
                               ret_p, conv_p)):
            outs[i].append(a)

        u, qb, kt_s, vt_s, qc, kc, vc, gc = _inproj(xs, w_in_b, l, kt_s, vt_s, ns)
        ya, s5r, s5i = _s5_branch(u.reshape(ns, S5_WIDTH), state_s5_re[l].reshape(ns, S5_FLAT),
                                  state_s5_im[l].reshape(ns, S5_FLAT), a_re, a_im, bd, cd,
                                  s5_d[l], w_glu_b, norm_a_g[l], steps=1)
        yb = _sb_sample(qb.reshape(ns, 1, SB_WIDTH), ck_pages, cv_pages, l, page_table,
                        sb_logit_bias[l], norm_b_g[l])
        yc, ret_s = _ret_sample(qc[0], kc[0], vc[0], gc[0], cos_s, sin_s, state_ret, l, ret_s,
                                ret_gn_g[l], ret_gn_b[l])
        x1 = _outproj(ya.reshape(1, ns, S5_WIDTH), yb.reshape(1, ns, SB_WIDTH), yc.reshape(1, ns, RET_WIDTH),
                      xs, w_out_b, l, ln1_g[l], ln1_b[l], ns)
        x2, conv_s = _ffn_sample(x1[0], state_conv[l], w_up_b, ffn_conv_w[l], cb, w_down_b, l,
                                 ln2_g[l], ln2_b[l], 1408)
        xs = x2.reshape(1, ns, d)
        for i, a in enumerate((s5r.reshape(ns, S5_GROUPS, S5_STATE), s5i.reshape(ns, S5_GROUPS, S5_STATE), conv_s)):
            outs[4 + i].append(a)

    stk = lambda i: jnp.stack(outs[i], axis=0)
    return (xp, xs.reshape(ns, 1, d),
            token_major(kt_p), token_major(vt_p),
            token_major(kt_s).reshape(DEPTH, ns, 1, SB_HEADS, SB_HEAD_DIM),
            token_major(vt_s).reshape(DEPTH, ns, 1, SB_HEADS, SB_HEAD_DIM),
            stk(0), stk(1), stk(4), stk(5),
            stk(2), ret_s,
            stk(3), stk(6))
```

```python
import functools
import math

import jax
import jax.numpy as jnp
from jax import lax
from jax.experimental import pallas as pl
from jax.experimental.pallas import tpu as pltpu

F32 = jnp.float32
BF16 = jnp.bfloat16

D_MODEL = 1024
DEPTH = 4
PAGE_SIZE = 128
S5_WIDTH = 256
S5_GROUP = 16
S5_GROUPS = 16
S5_STATE = 64
S5_FLAT = S5_GROUPS * S5_STATE
SB_HEADS = 4
SB_HEAD_DIM = 64
SB_WIDTH = 256
RET_WIDTH = 512
RET_HEADS = 4
RET_DV = 128
RET_DK = 64
RET_QK = 256
ROPE_BASE = 10000.0
D_FF = 2816
CONV_W = 3
LN_EPS = 1e-5
RMS_EPS = 1e-6
ALPHA = (2.0 * DEPTH) ** 0.25
SPLITS = (S5_WIDTH, SB_WIDTH, SB_WIDTH, SB_WIDTH, RET_QK, RET_QK, RET_WIDTH, RET_WIDTH)
D_IN = sum(SPLITS)
RET_LOG_DECAY = tuple(math.log1p(-(2.0 ** (-5.0 - h))) for h in range(RET_HEADS))

VMEM_LIMIT = 56 * 1024 * 1024


def _cparams(*sem):
    return pltpu.CompilerParams(dimension_semantics=sem, vmem_limit_bytes=VMEM_LIMIT)


def _dot(a, b):
    return jnp.dot(a, b, preferred_element_type=F32)


def _dot_nt(a, b):
    return lax.dot_general(a, b, (((1,), (1,)), ((), ())), preferred_element_type=F32)


def _dot_tn(a, b):
    return lax.dot_general(a, b, (((0,), (0,)), ((), ())), preferred_element_type=F32)


def _layer_norm(x, g, b):
    mu = jnp.mean(x, -1, keepdims=True)
    xc = x - mu
    var = jnp.mean(xc * xc, -1, keepdims=True)
    return xc * lax.rsqrt(var + LN_EPS) * g + b


def _rms_norm(x, g):
    return x * lax.rsqrt(jnp.mean(x * x, -1, keepdims=True) + RMS_EPS) * g


def _sigmoid(x):
    return 1.0 / (1.0 + jnp.exp(-x))


def _gelu_tanh(x):
    c = math.sqrt(2.0 / math.pi)
    return 0.5 * x * (1.0 + jnp.tanh(c * (x + 0.044715 * (x * x * x))))


def _full(shape):
    return pl.BlockSpec(shape, lambda *_: (0,) * len(shape))


def _rows(tl, w):
    return pl.BlockSpec((None, tl, w), lambda b, t: (b, t, 0))


def _ln_kernel(x_ref, g_ref, b_ref, o_ref):
    o_ref[...] = _layer_norm(x_ref[...], g_ref[...], b_ref[...])


def _input_ln(x, g, b, tl):
    bn, L, d = x.shape
    return pl.pallas_call(
        _ln_kernel,
        grid=(bn, L // tl),
        in_specs=[_rows(tl, d), _full((1, d)), _full((1, d))],
        out_specs=_rows(tl, d),
        out_shape=jax.ShapeDtypeStruct(x.shape, F32),
        compiler_params=_cparams("parallel", "parallel"),
        name="input_ln",
    )(x, g.reshape(1, d), b.reshape(1, d))


def _rope_table_kernel(pos_ref, inv_ref, cos_ref, sin_ref):
    ang = pos_ref[...] * inv_ref[...]
    lane = lax.broadcasted_iota(jnp.int32, ang.shape, 1)
    first = (lane % RET_DK) < (RET_DK // 2)
    cos_ref[...] = jnp.cos(ang)
    s = jnp.sin(ang)
    sin_ref[...] = jnp.where(first, -s, s)


def _rope_tables(pos):
    n = pos.shape[0]
    half = RET_DK // 2
    inv = ROPE_BASE ** (-jnp.arange(half, dtype=F32) / half)
    inv = jnp.tile(inv, 2 * RET_HEADS).reshape(1, RET_QK)
    return pl.pallas_call(
        _rope_table_kernel,
        out_shape=(jax.ShapeDtypeStruct((n, RET_QK), F32),) * 2,
        name="rope_tables",
    )(pos.reshape(n, 1), inv)


def _rope(x, cosf, sinf):
    lane = lax.broadcasted_iota(jnp.int32, (1, 128), 1)
    first = (lane % RET_DK) < (RET_DK // 2)
    halves = []
    for i in range(2):
        sl = slice(i * 128, (i + 1) * 128)
        xh = x[:, sl]
        partner = jnp.where(first, pltpu.roll(xh, 96, 1), pltpu.roll(xh, 32, 1))
        halves.append(xh * cosf[:, sl] + partner * sinf[:, sl])
    return jnp.concatenate(halves, axis=1)


def _s5_disc_kernel(lre_ref, lim_ref, ls_ref, bre_ref, bim_ref, are_ref, aim_ref, bbre_ref, bbim_ref):
    lam_re = jnp.minimum(lre_ref[...], -1e-4)
    lam_im = lim_ref[...]
    dt = jnp.exp(ls_ref[...])
    ldt_re, ldt_im = lam_re * dt, lam_im * dt
    mag = jnp.exp(ldt_re)
    a_re, a_im = mag * jnp.cos(ldt_im), mag * jnp.sin(ldt_im)
    den = lam_re * lam_re + lam_im * lam_im
    n_re, n_im = a_re - 1.0, a_im
    f_re = (n_re * lam_re + n_im * lam_im) / den
    f_im = (n_im * lam_re - n_re * lam_im) / den
    b_re, b_im = bre_ref[...], bim_ref[...]
    are_ref[...] = a_re
    aim_ref[...] = a_im
    bbre_ref[...] = f_re * b_re - f_im * b_im
    bbim_ref[...] = f_re * b_im + f_im * b_re


def _s5_params(lam_re, lam_im, log_step, b_re, b_im, c_re, c_im):
    G, P, H = S5_GROUPS, S5_STATE, S5_GROUP
    col = lambda a: a.reshape(G * P, 1)
    ls = jnp.broadcast_to(log_step[:, None], (G, P))
    a_re, a_im, bb_re, bb_im = pl.pallas_call(
        _s5_disc_kernel,
        out_shape=(jax.ShapeDtypeStruct((G * P, 1), F32),) * 2 + (jax.ShapeDtypeStruct((G * P, H), F32),) * 2,
        name="s5_discretize",
    )(col(lam_re), col(lam_im), col(ls), b_re.reshape(G * P, H), b_im.reshape(G * P, H))
    eye = jnp.eye(G, dtype=F32)

    def bdiag(bb):
        t = bb.reshape(G, P, H).transpose(0, 2, 1)
        return (t[:, :, None, :] * eye[:, None, :, None]).reshape(G * H, G * P)

    def cdiag(c):
        t = c.transpose(0, 2, 1)
        return (t[:, :, None, :] * eye[:, None, :, None]).reshape(G * P, G * H)

    bd = jnp.concatenate([bdiag(bb_re), bdiag(bb_im)], axis=1).astype(BF16)
    cd = jnp.concatenate([cdiag(c_re), -cdiag(c_im)], axis=0).astype(BF16)
    return a_re.reshape(1, G * P), a_im.reshape(1, G * P), bd, cd


def _inproj_kernel(x_ref, w_ref, *refs, n_alias):
    out_refs = refs[n_alias:]
    xb = x_ref[...].astype(BF16)
    off = 0
    for i, (ref, wd) in enumerate(zip(out_refs, SPLITS)):
        h = _dot(xb, w_ref[:, off:off + wd])
        ref[...] = h.T if i in (2, 3) else h
        off += wd


def _inproj(x, w_all, layer, kt_buf, vt_buf, tl):
    bn, L, d = x.shape
    alias = [] if kt_buf is None else [kt_buf, vt_buf]
    out_specs, out_shape = [], []
    for i, wd in enumerate(SPLITS):
        if i in (2, 3):
            out_specs.append(pl.BlockSpec((None, None, wd, tl), lambda b, t: (layer, b, 0, t)))
            out_shape.append(jax.ShapeDtypeStruct((DEPTH, bn, wd, L), F32))
        else:
            out_specs.append(_rows(tl, wd))
            out_shape.append(jax.ShapeDtypeStruct((bn, L, wd), F32))
    return pl.pallas_call(
        functools.partial(_inproj_kernel, n_alias=len(alias)),
        grid=(bn, L // tl),
        in_specs=[_rows(tl, d), pl.BlockSpec((None, d, D_IN), lambda b, t: (layer, 0, 0))]
        + [pl.BlockSpec(memory_space=pl.ANY)] * len(alias),
        out_specs=out_specs,
        out_shape=out_shape,
        input_output_aliases={2 + i: 2 + i for i in range(len(alias))},
        compiler_params=_cparams("parallel", "parallel"),
        name="inproj",
    )(x, w_all, *alias)


def _s5_kernel(u_ref, bd_ref, cd_ref, are_ref, aim_ref, d_ref, wglu_ref, g_ref, x0r_ref, x0i_ref,
               ya_ref, xr_out, xi_out, bu_scr, xs_scr, st_re, st_im, *, steps, bs):
    c = pl.program_id(0)

    @pl.when(c == 0)
    def _():
        st_re[...] = x0r_ref[...]
        st_im[...] = x0i_ref[...]

    u = u_ref[...]
    bu_scr[...] = _dot(u.astype(BF16), bd_ref[...])
    a_re = jnp.broadcast_to(are_ref[...], (bs, S5_FLAT))
    a_im = jnp.broadcast_to(aim_ref[...], (bs, S5_FLAT))

    def step(t, carry):
        xr, xi = carry
        r0 = pl.multiple_of(t * bs, bs)
        br = bu_scr[pl.ds(r0, bs), 0:S5_FLAT]
        bi = bu_scr[pl.ds(r0, bs), S5_FLAT:2 * S5_FLAT]
        nr = a_re * xr - a_im * xi + br
        ni = a_re * xi + a_im * xr + bi
        xs_scr[pl.ds(r0, bs), 0:S5_FLAT] = nr
        xs_scr[pl.ds(r0, bs), S5_FLAT:2 * S5_FLAT] = ni
        return nr, ni

    xr, xi = lax.fori_loop(0, steps, step, (st_re[...], st_im[...]))
    st_re[...] = xr
    st_im[...] = xi
    xr_out[...] = xr
    xi_out[...] = xi

    y = _dot(xs_scr[...].astype(BF16), cd_ref[...]) + d_ref[...] * u
    h = _gelu_tanh(y)
    out = h * _sigmoid(_dot(h.astype(BF16), wglu_ref[...]))
    ya_ref[...] = _rms_norm(out, g_ref[...])


def _s5_branch(u_tm, x0_re, x0_im, a_re, a_im, bd, cd, d, w_glu, g, steps):
    bs = x0_re.shape[0]
    rows = u_tm.shape[0]
    n_chunks = rows // (steps * bs)
    blk = steps * bs
    kern = functools.partial(_s5_kernel, steps=steps, bs=bs)
    vec = lambda n: _full((1, n))
    return pl.pallas_call(
        kern,
        grid=(n_chunks,),
        in_specs=[pl.BlockSpec((blk, S5_WIDTH), lambda c: (c, 0)),
                  _full(bd.shape), _full(cd.shape), vec(S5_FLAT), vec(S5_FLAT), vec(S5_WIDTH),
                  _full((S5_WIDTH, S5_WIDTH)), vec(S5_WIDTH), _full((bs, S5_FLAT)), _full((bs, S5_FLAT))],
        out_specs=[pl.BlockSpec((blk, S5_WIDTH), lambda c: (c, 0)), _full((bs, S5_FLAT)), _full((bs, S5_FLAT))],
        out_shape=[jax.ShapeDtypeStruct((rows, S5_WIDTH), F32),
                   jax.ShapeDtypeStruct((bs, S5_FLAT), F32), jax.ShapeDtypeStruct((bs, S5_FLAT), F32)],
        scratch_shapes=[pltpu.VMEM((blk, 2 * S5_FLAT), F32), pltpu.VMEM((blk, 2 * S5_FLAT), F32),
                        pltpu.VMEM((bs, S5_FLAT), F32), pltpu.VMEM((bs, S5_FLAT), F32)],
        compiler_params=_cparams("arbitrary"),
        name="s5_branch",
    )(u_tm, bd, cd, a_re, a_im, d.reshape(1, -1), w_glu, g.reshape(1, -1), x0_re, x0_im)


def _sb_logits(z):
    t = jnp.log(1.0 + jnp.exp2(jnp.abs(z) * (-1.0 / math.log(2.0))))
    lb = jnp.minimum(z, 0.0) - t
    return lb, lb - z


def _sb_prompt_kernel(bias_ref, q_ref, k_ref, v_ref, g_ref, o_ref, kb_scr, vb_scr, acc_scr, *, tq):
    qi = pl.program_id(1)

    @pl.when(qi == 0)
    def _():
        for jj in range(kb_scr.shape[0]):
            kb_scr[jj] = k_ref[:, jj * tq:(jj + 1) * tq].astype(BF16)
            vb_scr[jj] = v_ref[:, jj * tq:(jj + 1) * tq].astype(BF16)

    rows = SB_HEADS * tq
    q = q_ref[...] * (SB_HEAD_DIM ** -0.5)
    lane = lax.broadcasted_iota(jnp.int32, (1, SB_WIDTH), 1)
    q4 = jnp.concatenate([jnp.where(lane // SB_HEAD_DIM == h, q, 0.0) for h in range(SB_HEADS)],
                         axis=0).astype(BF16)
    rblock = lax.broadcasted_iota(jnp.int32, (rows, 1), 0) // tq
    bias = jnp.zeros((rows, 1), F32)
    for h in range(SB_HEADS):
        bias = jnp.where(rblock == h, bias_ref[h], bias)
    qpos = lax.broadcasted_iota(jnp.int32, (rows, tq), 0) % tq
    kpos = lax.broadcasted_iota(jnp.int32, (rows, tq), 1)
    causal = kpos < qpos
    later = (lax.broadcasted_iota(jnp.int32, (tq, tq), 0)
             > lax.broadcasted_iota(jnp.int32, (tq, tq), 1)).astype(BF16)

    def tile(j, carry, masked):
        z = _dot(q4, kb_scr[j]) + bias
        lb, lr = _sb_logits(z)
        if masked:
            lr = jnp.where(causal, lr, 0.0)
        tail = _dot(lr.astype(BF16), later) + carry
        w = jnp.exp(lb + tail)
        if masked:
            w = jnp.where(causal, w, 0.0)
        acc_scr[...] += _dot_nt(w.astype(BF16), vb_scr[j])
        return carry + jnp.sum(lr, axis=1, keepdims=True)

    def tile_pair(i, carry):
        ja = qi - 1 - 2 * i
        jb = ja - 1
        lb_a, lr_a = _sb_logits(_dot(q4, kb_scr[ja]) + bias)
        lb_b, lr_b = _sb_logits(_dot(q4, kb_scr[jb]) + bias)
        carry_b = carry + jnp.sum(lr_a, axis=1, keepdims=True)
        w_a = jnp.exp(lb_a + (_dot(lr_a.astype(BF16), later) + carry))
        w_b = jnp.exp(lb_b + (_dot(lr_b.astype(BF16), later) + carry_b))
        acc_scr[...] += _dot_nt(w_a.astype(BF16), vb_scr[ja]) + _dot_nt(w_b.astype(BF16), vb_scr[jb])
        return carry_b + jnp.sum(lr_b, axis=1, keepdims=True)

    acc_scr[...] = jnp.zeros_like(acc_scr)
    carry = tile(qi, jnp.zeros((rows, 1), F32), True)
    carry = lax.fori_loop(0, lax.shift_right_logical(qi, 1), tile_pair, carry)

    @pl.when((qi & 1) == 1)
    def _():
        tile(0, carry, False)

    out = jnp.zeros((tq, SB_WIDTH), F32)
    for h in range(SB_HEADS):
        out = jnp.where(lane // SB_HEAD_DIM == h, acc_scr[h * tq:(h + 1) * tq, :], out)
    o_ref[...] = _rms_norm(out, g_ref[...])


def _sb_prompt(q, kt, vt, layer, bias, g, tq):
    bn, L, w = q.shape
    kern = functools.partial(_sb_prompt_kernel, tq=tq)
    seq = pl.BlockSpec((None, None, w, L), lambda b, t: (layer, b, 0, 0))
    return pl.pallas_call(
        kern,
        grid=(bn, L // tq),
        in_specs=[pl.BlockSpec(memory_space=pltpu.SMEM), _rows(tq, w), seq, seq, _full((1, w))],
        out_specs=_rows(tq, w),
        out_shape=jax.ShapeDtypeStruct((bn, L, w), F32),
        scratch_shapes=[pltpu.VMEM((L // tq, w, tq), BF16), pltpu.VMEM((L // tq, w, tq), BF16),
                        pltpu.VMEM((SB_HEADS * tq, w), F32)],
        compiler_params=_cparams("parallel", "arbitrary"),
        name="sb_prompt",
    )(bias, q, kt, vt, g.reshape(1, w))


def _sb_sample_kernel(pt_ref, bias_ref, q_ref, *refs, n_pages):
    del pt_ref
    k_refs, v_refs = refs[:n_pages], refs[n_pages:2 * n_pages]
    g_ref, o_ref = refs[2 * n_pages], refs[2 * n_pages + 1]
    sub = lax.broadcasted_iota(jnp.int32, (8, SB_WIDTH), 0)
    lane = lax.broadcasted_iota(jnp.int32, (8, SB_WIDTH), 1)
    diag = (lane // SB_HEAD_DIM) == sub
    q = q_ref[...] * (SB_HEAD_DIM ** -0.5)
    qblk = jnp.where(diag, jnp.broadcast_to(q, (8, SB_WIDTH)), 0.0).astype(BF16)
    sub1 = lax.broadcasted_iota(jnp.int32, (8, 1), 0)
    bias = jnp.zeros((8, 1), F32)
    for h in range(SB_HEADS):
        bias = jnp.where(sub1 == h, bias_ref[h], bias)

    lbs, lrs = [], []
    for p in range(n_pages):
        z = _dot(qblk, k_refs[p][...].astype(BF16)) + bias
        lb, lr = _sb_logits(z)
        lbs.append(lb)
        lrs.append(lr)

    r = lax.broadcasted_iota(jnp.int32, (PAGE_SIZE, PAGE_SIZE), 0)
    c = lax.broadcasted_iota(jnp.int32, (PAGE_SIZE, PAGE_SIZE), 1)
    later = (r > c).astype(BF16)
    lr_all = jnp.concatenate(lrs, axis=0)
    hi = lr_all.astype(BF16)
    lo = (lr_all - hi.astype(F32)).astype(BF16)
    tail_in_page = _dot(hi, later) + _dot(lo, later)

    carry = jnp.zeros((8, 1), F32)
    acc = jnp.zeros((8, SB_WIDTH), F32)
    for p in reversed(range(n_pages)):
        w = jnp.exp(lbs[p] + tail_in_page[8 * p:8 * p + 8] + carry)
        acc = acc + _dot_nt(w.astype(BF16), v_refs[p][...].astype(BF16))
        carry = carry + jnp.sum(lrs[p], axis=1, keepdims=True)
    out = jnp.sum(jnp.where(diag, acc, 0.0), axis=0, keepdims=True)
    o_ref[...] = _rms_norm(out, g_ref[...])


def _sb_sample(q, cache_k, cache_v, layer, page_table, bias, g):
    bn = q.shape[0]
    n_pages = page_table.shape[1]
    kern = functools.partial(_sb_sample_kernel, n_pages=n_pages)

    def page_spec(p):
        return pl.BlockSpec((None, None, SB_WIDTH, PAGE_SIZE), lambda b, pt: (layer, pt[b, p], 0, 0))

    row = pl.BlockSpec((None, 1, SB_WIDTH), lambda b, pt: (b, 0, 0))
    grid_spec = pltpu.PrefetchScalarGridSpec(
        num_scalar_prefetch=1,
        grid=(bn,),
        in_specs=[pl.BlockSpec(memory_space=pltpu.SMEM), row]
        + [page_spec(p) for p in range(n_pages)] * 2
        + [pl.BlockSpec((1, SB_WIDTH), lambda b, pt: (0, 0))],
        out_specs=row,
    )
    return pl.pallas_call(
        kern,
        grid_spec=grid_spec,
        out_shape=jax.ShapeDtypeStruct((bn, 1, SB_WIDTH), F32),
        compiler_params=_cparams("arbitrary"),
        name="sb_sample",
    )(page_table, bias, q, *([cache_k] * n_pages), *([cache_v] * n_pages), g.reshape(1, -1))


def _head_lane_consts(width, per_head):
    lane = lax.broadcasted_iota(jnp.int32, (1, width), 1)
    lg = jnp.zeros((1, width), F32)
    for h in range(RET_HEADS):
        lg = jnp.where(lane // per_head == h, RET_LOG_DECAY[h], lg)
    return lg


def _group_norm_gate(o, g, gn_g, gn_b):
    parts = []
    for h in range(RET_HEADS):
        oh = o[:, h * RET_DV:(h + 1) * RET_DV]
        mu = jnp.mean(oh, -1, keepdims=True)
        oc = oh - mu
        var = jnp.mean(oc * oc, -1, keepdims=True)
        parts.append(oc * lax.rsqrt(var + LN_EPS))
    on = jnp.concatenate(parts, axis=1) * gn_g + gn_b
    return g * _sigmoid(g) * on


def _ret_prompt_kernel(q_ref, k_ref, v_ref, g_ref, cos_ref, sin_ref, gng_ref, gnb_ref, s0_ref,
                       o_ref, sout_ref, s_scr, dec_scr, gam_scr, *, ck):
    c = pl.program_id(1)
    rowi = lax.broadcasted_iota(jnp.int32, (RET_QK, RET_WIDTH), 0)
    coli = lax.broadcasted_iota(jnp.int32, (RET_QK, RET_WIDTH), 1)
    blockmask = (rowi // RET_DK) == (coli // RET_DV)

    @pl.when(c == 0)
    def _():
        s_scr[...] = jnp.zeros_like(s_scr)
        di = lax.broadcasted_iota(jnp.int32, (ck, ck), 0) - lax.broadcasted_iota(jnp.int32, (ck, ck), 1)
        df = jnp.maximum(di, 0).astype(F32)
        gam = jnp.zeros((RET_QK, RET_WIDTH), F32)
        for h in range(RET_HEADS):
            s_scr[h * RET_DK:(h + 1) * RET_DK, h * RET_DV:(h + 1) * RET_DV] = s0_ref[h]
            dec_scr[h] = jnp.where(di >= 0, jnp.exp(RET_LOG_DECAY[h] * df), 0.0)
            gam = jnp.where((rowi // RET_DK == h) & blockmask, math.exp(RET_LOG_DECAY[h] * ck), gam)
        gam_scr[...] = gam

    cosf, sinf = cos_ref[...], sin_ref[...]
    q = _rope(q_ref[...], cosf, sinf) * (RET_DK ** -0.5)
    k = _rope(k_ref[...], cosf, sinf)
    v = v_ref[...]
    kb, vb = k.astype(BF16), v.astype(BF16)
    lane = lax.broadcasted_iota(jnp.int32, (1, RET_QK), 1)
    idx = lax.broadcasted_iota(jnp.int32, (ck, 1), 0).astype(F32)
    lg = _head_lane_consts(RET_QK, RET_DK)

    intra = []
    for h in range(RET_HEADS):
        hm = (lane // RET_DK) == h
        inner = _dot_nt(jnp.where(hm, q, 0.0).astype(BF16), kb)
        p = (inner * dec_scr[h]).astype(BF16)
        intra.append(_dot(p, vb[:, h * RET_DV:(h + 1) * RET_DV]))
    q_dec = q * jnp.exp(lg * (idx + 1.0))
    o = jnp.concatenate(intra, axis=1) + _dot(q_dec.astype(BF16), s_scr[...].astype(BF16))
    k_dec = k * jnp.exp(lg * (ck - 1.0 - idx))
    kv = _dot_tn(k_dec.astype(BF16), vb)
    s_new = gam_scr[...] * s_scr[...] + jnp.where(blockmask, kv, 0.0)
    s_scr[...] = s_new
    for h in range(RET_HEADS):
        sout_ref[h] = s_new[h * RET_DK:(h + 1) * RET_DK, h * RET_DV:(h + 1) * RET_DV]
    o_ref[...] = _group_norm_gate(o, g_ref[...], gng_ref[...], gnb_ref[...])


def _ret_prompt(q, k, v, g, cosf, sinf, s0, gn_g, gn_b, ck):
    bn, L, _ = q.shape
    kern = functools.partial(_ret_prompt_kernel, ck=ck)
    tab = pl.BlockSpec((ck, RET_QK), lambda b, c: (c, 0))
    st = pl.BlockSpec((None, RET_HEADS, RET_DK, RET_DV), lambda b, c: (b, 0, 0, 0))
    return pl.pallas_call(
        kern,
        grid=(bn, L // ck),
        in_specs=[_rows(ck, RET_QK), _rows(ck, RET_QK), _rows(ck, RET_WIDTH), _rows(ck, RET_WIDTH),
                  tab, tab, _full((1, RET_WIDTH)), _full((1, RET_WIDTH)), st],
        out_specs=[_rows(ck, RET_WIDTH), st],
        out_shape=[jax.ShapeDtypeStruct((bn, L, RET_WIDTH), F32),
                   jax.ShapeDtypeStruct((bn, RET_HEADS, RET_DK, RET_DV), F32)],
        scratch_shapes=[pltpu.VMEM((RET_QK, RET_WIDTH), F32), pltpu.VMEM((RET_HEADS, ck, ck), F32),
                        pltpu.VMEM((RET_QK, RET_WIDTH), F32)],
        compiler_params=_cparams("parallel", "arbitrary"),
        name="ret_prompt",
    )(q, k, v, g, cosf, sinf, gn_g.reshape(1, -1), gn_b.reshape(1, -1), s0)


def _ret_sample_kernel(q_ref, k_ref, v_ref, g_ref, cos_ref, sin_ref, gng_ref, gnb_ref, s_ref, *refs, gb):
    o_ref, sout_ref, qt_scr, kt_scr = refs[-4:]
    i = pl.program_id(0)
    nb = q_ref.shape[0]

    @pl.when(i == 0)
    def _():
        cosf = jnp.broadcast_to(cos_ref[...], (nb, RET_QK))
        sinf = jnp.broadcast_to(sin_ref[...], (nb, RET_QK))
        qt_scr[...] = (_rope(q_ref[...], cosf, sinf) * (RET_DK ** -0.5)).T
        kt_scr[...] = _rope(k_ref[...], cosf, sinf).T

    lane = lax.broadcasted_iota(jnp.int32, (1, nb), 1)
    for j in range(gb):
        b = i * gb + j
        onehot = lane == b
        qcol = jnp.sum(jnp.where(onehot, qt_scr[...], 0.0), axis=1, keepdims=True)
        kcol = jnp.sum(jnp.where(onehot, kt_scr[...], 0.0), axis=1, keepdims=True)
        parts = []
        for h in range(RET_HEADS):
            s = s_ref[j, h]
            qc = qcol[h * RET_DK:(h + 1) * RET_DK]
            kc = kcol[h * RET_DK:(h + 1) * RET_DK]
            vrow = v_ref[j:j + 1, h * RET_DV:(h + 1) * RET_DV]
            gamma = math.exp(RET_LOG_DECAY[h])
            qk = jnp.sum(qc * kc, axis=0, keepdims=True)
            parts.append(qk * vrow + jnp.sum((qc * gamma) * s, axis=0, keepdims=True))
            sout_ref[j, h] = gamma * s + kc * vrow
        o = jnp.concatenate(parts, axis=1)
        o_ref[j:j + 1, :] = _group_norm_gate(o, g_ref[j:j + 1, :], gng_ref[...], gnb_ref[...])


def _ret_sample(q, k, v, g, cosf, sinf, s_all, layer, s_buf, gn_g, gn_b, gb=8):
    nb = q.shape[0]
    alias = [] if s_buf is None else [s_buf]
    kern = functools.partial(_ret_sample_kernel, gb=gb)
    st = pl.BlockSpec((None, gb, RET_HEADS, RET_DK, RET_DV), lambda i: (layer, i, 0, 0, 0))
    return pl.pallas_call(
        kern,
        grid=(nb // gb,),
        in_specs=[_full((nb, RET_QK)), _full((nb, RET_QK)),
                  pl.BlockSpec((gb, RET_WIDTH), lambda i: (i, 0)), pl.BlockSpec((gb, RET_WIDTH), lambda i: (i, 0)),
                  _full((1, RET_QK)), _full((1, RET_QK)), _full((1, RET_WIDTH)), _full((1, RET_WIDTH)), st]
        + [pl.BlockSpec(memory_space=pl.ANY)] * len(alias),
        out_specs=[pl.BlockSpec((gb, RET_WIDTH), lambda i: (i, 0)), st],
        out_shape=[jax.ShapeDtypeStruct((nb, RET_WIDTH), F32), jax.ShapeDtypeStruct(s_all.shape, F32)],
        input_output_aliases={9: 1} if alias else {},
        scratch_shapes=[pltpu.VMEM((RET_QK, nb), F32), pltpu.VMEM((RET_QK, nb), F32)],
        compiler_params=_cparams("arbitrary"),
        name="ret_sample",
    )(q, k, v, g, cosf, sinf, gn_g.reshape(1, -1), gn_b.reshape(1, -1), s_all, *alias)


def _outproj_kernel(ya_ref, yb_ref, yc_ref, x_ref, w_ref, g_ref, b_ref, o_ref):
    mix = (_dot(ya_ref[...].astype(BF16), w_ref[0:S5_WIDTH, :])
           + _dot(yb_ref[...].astype(BF16), w_ref[S5_WIDTH:S5_WIDTH + SB_WIDTH, :])
           + _dot(yc_ref[...].astype(BF16), w_ref[S5_WIDTH + SB_WIDTH:, :]))
    o_ref[...] = _layer_norm(ALPHA * x_ref[...] + mix, g_ref[...], b_ref[...])


def _outproj(ya, yb, yc, x, w_all, layer, g, b, tl):
    bn, L, d = x.shape
    return pl.pallas_call(
        _outproj_kernel,
        grid=(bn, L // tl),
        in_specs=[_rows(tl, S5_WIDTH), _rows(tl, SB_WIDTH), _rows(tl, RET_WIDTH), _rows(tl, d),
                  pl.BlockSpec((None, d, d), lambda bi, t: (layer, 0, 0)), _full((1, d)), _full((1, d))],
        out_specs=_rows(tl, d),
        out_shape=jax.ShapeDtypeStruct((bn, L, d), F32),
        compiler_params=_cparams("parallel", "parallel"),
        name="outproj_ln",
    )(ya, yb, yc, x, w_all, g.reshape(1, d), b.reshape(1, d))


FFN_SUB = 1408


def _conv_taps(h, hm1, hm2, cw_ref, cb_ref):
    return cb_ref[...] + (cw_ref[0:1, :] * hm2 + cw_ref[1:2, :] * hm1 + cw_ref[2:3, :] * h)


def _ffn_prompt_kernel(x_ref, wua_ref, wug_ref, cwa_ref, cwg_ref, cba_ref, cbg_ref, wd_ref, pa_ref, pg_ref,
                       g_ref, b_ref, o_ref, ca_ref, cg_ref, acc_scr, prev_scr, carry_scr, *, tl, nf, fc):
    t = pl.program_id(1)
    c = pl.program_id(2)

    @pl.when(c == 0)
    def _():
        acc_scr[...] = jnp.zeros_like(acc_scr)

    for part, prev_ref in ((0, pa_ref), (1, pg_ref)):
        @pl.when(t == 0)
        def _(part=part, prev_ref=prev_ref):
            prev_scr[part] = prev_ref[...]

        @pl.when(t > 0)
        def _(part=part):
            prev_scr[part] = carry_scr[c, part]

    x = x_ref[...]
    xb = x.astype(BF16)
    row8 = lax.broadcasted_iota(jnp.int32, (8, 1), 0)

    def conv(h, part, lo, hi, cw_ref, cb_ref):
        p0 = prev_scr[part, 0:1, lo:hi]
        p1 = prev_scr[part, 1:2, lo:hi]
        r1 = pltpu.roll(h, 1, 0)
        r2 = pltpu.roll(h, 2, 0)
        hm1 = jnp.concatenate([jnp.where(row8 == 0, p1, r1[0:8]), r1[8:]], axis=0)
        hm2 = jnp.concatenate([jnp.where(row8 == 0, p0, jnp.where(row8 == 1, p1, r2[0:8])), r2[8:]], axis=0)
        return cb_ref[:, lo:hi] + (cw_ref[0:1, lo:hi] * hm2 + cw_ref[1:2, lo:hi] * hm1 + cw_ref[2:3, lo:hi] * h)

    bounds = [(lo, min(lo + FFN_SUB, fc)) for lo in range(0, fc, FFN_SUB)]
    up = lambda k: (_dot(xb, wua_ref[:, bounds[k][0]:bounds[k][1]]), _dot(xb, wug_ref[:, bounds[k][0]:bounds[k][1]]))
    nxt = up(0)
    acc = None
    for k, (lo, hi) in enumerate(bounds):
        ha, hg = nxt
        if k + 1 < len(bounds):
            nxt = up(k + 1)
        ca = conv(ha, 0, lo, hi, cwa_ref, cba_ref)
        cg = conv(hg, 1, lo, hi, cwg_ref, cbg_ref)
        act = (cg * _sigmoid(cg) * ca).astype(BF16)
        down = _dot(act, wd_ref[lo:hi, :])
        acc = down if acc is None else acc + down
        for part, h, conv_ref in ((0, ha, ca_ref), (1, hg, cg_ref)):
            last2 = h[tl - 2:tl, :]
            carry_scr[c, part, :, lo:hi] = last2
            conv_ref[:, lo:hi] = last2
    acc_scr[...] += acc

    @pl.when(c == nf - 1)
    def _():
        o_ref[...] = _layer_norm(ALPHA * x + acc_scr[...], g_ref[...], b_ref[...])


def _ffn_prompt(x, prev, w_up_all, conv_w, conv_b, w_down_all, layer, g, b, tl, fc):
    bn, L, d = x.shape
    nf = D_FF // fc
    kern = functools.partial(_ffn_prompt_kernel, tl=tl, nf=nf, fc=fc)
    colblk = lambda rows, off: pl.BlockSpec((rows, fc), lambda bi, t, c: (0, c + off))
    wublk = lambda off: pl.BlockSpec((None, d, fc), lambda bi, t, c: (layer, 0, c + off))
    prevblk = lambda off: pl.BlockSpec((None, 2, fc), lambda bi, t, c: (bi, 0, c + off))
    xspec = pl.BlockSpec((None, tl, d), lambda bi, t, c: (bi, t, 0))
    vec = pl.BlockSpec((1, d), lambda bi, t, c: (0, 0))
    convout = pl.BlockSpec((None, None, 2, fc), lambda bi, t, c: (bi, t, 0, c))
    y, ca, cg = pl.pallas_call(
        kern,
        grid=(bn, L // tl, nf),
        in_specs=[xspec, wublk(0), wublk(nf), colblk(CONV_W, 0), colblk(CONV_W, nf),
                  colblk(1, 0), colblk(1, nf), pl.BlockSpec((None, fc, d), lambda bi, t, c: (layer, c, 0)),
                  prevblk(0), prevblk(nf), vec, vec],
        out_specs=[xspec, convout, convout],
        out_shape=[jax.ShapeDtypeStruct((bn, L, d), F32),
                   jax.ShapeDtypeStruct((bn, L // tl, 2, D_FF), F32),
                   jax.ShapeDtypeStruct((bn, L // tl, 2, D_FF), F32)],
        scratch_shapes=[pltpu.VMEM((tl, d), F32), pltpu.VMEM((2, 2, fc), F32),
                        pltpu.VMEM((nf, 2, 2, fc), F32)],
        compiler_params=_cparams("parallel", "arbitrary", "arbitrary"),
        name="ffn_prompt",
    )(x, w_up_all, w_up_all, conv_w, conv_w, conv_b, conv_b, w_down_all, prev, prev,
      g.reshape(1, d), b.reshape(1, d))
    return y, jnp.concatenate([ca[:, -1], cg[:, -1]], axis=-1)


def _ffn_sample_kernel(x_ref, wua_ref, wug_ref, cwa_ref, cwg_ref, cba_ref, cbg_ref, wd_ref,
                       p0a_ref, p0g_ref, p1a_ref, p1g_ref, g_ref, b_ref, o_ref, ha_ref, hg_ref, acc_scr, *, nf):
    c = pl.program_id(0)

    @pl.when(c == 0)
    def _():
        acc_scr[...] = jnp.zeros_like(acc_scr)

    x = x_ref[...]
    xb = x.astype(BF16)
    ha = _dot(xb, wua_ref[...])
    hg = _dot(xb, wug_ref[...])
    ha_ref[...] = ha
    hg_ref[...] = hg
    ca = _conv_taps(ha, p1a_ref[...], p0a_ref[...], cwa_ref, cba_ref)
    cg = _conv_taps(hg, p1g_ref[...], p0g_ref[...], cwg_ref, cbg_ref)
    act = (cg * _sigmoid(cg) * ca).astype(BF16)
    acc_scr[...] += _dot(act, wd_ref[...])

    @pl.when(c == nf - 1)
    def _():
        o_ref[...] = _layer_norm(ALPHA * x + acc_scr[...], g_ref[...], b_ref[...])


def _ffn_sample(x, prev, w_up_all, conv_w, conv_b, w_down_all, layer, g, b, fc):
    n, d = x.shape
    nf = D_FF // fc
    kern = functools.partial(_ffn_sample_kernel, nf=nf)
    colblk = lambda rows, off: pl.BlockSpec((rows, fc), lambda c: (0, c + off))
    wublk = lambda off: pl.BlockSpec((None, d, fc), lambda c: (layer, 0, c + off))
    prevblk = lambda j, off: pl.BlockSpec((n, fc), lambda c: (0, j * 2 * nf + c + off))
    prev2d = prev.reshape(n, (CONV_W - 1) * 2 * D_FF)
    full = lambda shape: pl.BlockSpec(shape, lambda c: (0, 0))
    hout = pl.BlockSpec((n, fc), lambda c: (0, c))
    y, ha, hg = pl.pallas_call(
        kern,
        grid=(nf,),
        in_specs=[full((n, d)), wublk(0), wublk(nf), colblk(CONV_W, 0), colblk(CONV_W, nf),
                  colblk(1, 0), colblk(1, nf), pl.BlockSpec((None, fc, d), lambda c: (layer, c, 0)),
                  prevblk(0, 0), prevblk(0, nf), prevblk(1, 0), prevblk(1, nf), full((1, d)), full((1, d))],
        out_specs=[full((n, d)), hout, hout],
        out_shape=[jax.ShapeDtypeStruct((n, d), F32),
                   jax.ShapeDtypeStruct((n, D_FF), F32), jax.ShapeDtypeStruct((n, D_FF), F32)],
        scratch_shapes=[pltpu.VMEM((n, d), F32)],
        compiler_params=_cparams("arbitrary"),
        name="ffn_sample",
    )(x, w_up_all, w_up_all, conv_w, conv_w, conv_b, conv_b, w_down_all, prev2d, prev2d, prev2d, prev2d,
      g.reshape(1, d), b.reshape(1, d))
    h_new = jnp.concatenate([ha, hg], axis=-1)
    return y, jnp.stack([prev[:, 1], h_new], axis=1)


def kernel(x_prompt, x_sample, cache_k, cache_v, state_s5_re, state_s5_im, state_ret, state_conv, page_table, ln_in_g, ln_in_b, w_in, s5_lambda_re, s5_lambda_im, s5_log_step, s5_b_re, s5_b_im, s5_c_re, s5_c_im, s5_d, s5_w_glu, norm_a_g, norm_b_g, sb_logit_bias, ret_gn_g, ret_gn_b, w_out, ln1_g, ln1_b, ffn_w_up, ffn_conv_w, ffn_conv_b, ffn_w_down, ln2_g, ln2_b):
    bp, lp, d = x_prompt.shape
    ns = x_sample.shape[0]
    past = page_table.shape[1] * PAGE_SIZE
    n_pool = cache_k.shape[1]
    ck_pages = cache_k.transpose(0, 1, 3, 4, 2).reshape(DEPTH, n_pool, SB_WIDTH, PAGE_SIZE)
    cv_pages = cache_v.transpose(0, 1, 3, 4, 2).reshape(DEPTH, n_pool, SB_WIDTH, PAGE_SIZE)

    def token_major(t):
        _, bn, _, L = t.shape
        return t.reshape(DEPTH, bn, SB_HEADS, SB_HEAD_DIM, L).transpose(0, 1, 4, 2, 3)

    w_in_b = w_in.astype(BF16)
    w_out_b = w_out.astype(BF16)
    w_up_b = ffn_w_up.astype(BF16)
    w_down_b = ffn_w_down.astype(BF16)
    kt_p = vt_p = kt_s = vt_s = ret_s = None

    xp = _input_ln(x_prompt, ln_in_g, ln_in_b, 1024)
    xs = _input_ln(x_sample.reshape(1, ns, d), ln_in_g, ln_in_b, ns)
    cos_p, sin_p = _rope_tables(jnp.arange(lp, dtype=F32))
    cos_s, sin_s = _rope_tables(jnp.full((8,), past, F32))
    cos_s, sin_s = cos_s[0:1], sin_s[0:1]

    zero_s5 = jnp.zeros((bp, S5_FLAT), F32)
    zero_ret = jnp.zeros((bp, RET_HEADS, RET_DK, RET_DV), F32)
    zero_conv = jnp.zeros((bp, CONV_W - 1, 2 * D_FF), F32)
    outs = [[] for _ in range(8)]

    for l in range(DEPTH):
        w_glu_b = s5_w_glu[l].astype(BF16)
        cb = ffn_conv_b[l].reshape(1, -1)
        a_re, a_im, bd, cd = _s5_params(s5_lambda_re[l], s5_lambda_im[l], s5_log_step[l],
                                        s5_b_re[l], s5_b_im[l], s5_c_re[l], s5_c_im[l])

        u, qb, kt_p, vt_p, qc, kc, vc, gc = _inproj(xp, w_in_b, l, kt_p, vt_p, 512)
        ya, s5r, s5i = _s5_branch(u.transpose(1, 0, 2).reshape(lp * bp, S5_WIDTH), zero_s5, zero_s5, a_re, a_im, bd, cd,
                                  s5_d[l], w_glu_b, norm_a_g[l], steps=128)
        yb = _sb_prompt(qb, kt_p, vt_p, l, sb_logit_bias[l], norm_b_g[l], 256)
        yc, ret_p = _ret_prompt(qc, kc, vc, gc, cos_p, sin_p, zero_ret, ret_gn_g[l], ret_gn_b[l], 256)
        x1 = _outproj(ya.reshape(lp, bp, S5_WIDTH).transpose(1, 0, 2), yb, yc, xp, w_out_b, l,
                      ln1_g[l], ln1_b[l], 1024)
        xp, conv_p = _ffn_prompt(x1, zero_conv, w_up_b, ffn_conv_w[l], cb, w_down_b, l,
                                 ln2_g[l], ln2_b[l], 512, 1408)
        for i, a in enumerate((s5r.reshape(bp, S5_GROUPS, S5_STATE), s5i.reshape(bp, S5_GROUPS, S5_STATE),
                               ret_p, conv_p)):
            outs[i].append(a)

        u, qb, kt_s, vt_s, qc, kc, vc, gc = _inproj(xs, w_in_b, l, kt_s, vt_s, ns)
        ya, s5r, s5i = _s5_branch(u.reshape(ns, S5_WIDTH), state_s5_re[l].reshape(ns, S5_FLAT),
                                  state_s5_im[l].reshape(ns, S5_FLAT), a_re, a_im, bd, cd,
                                  s5_d[l], w_glu_b, norm_a_g[l], steps=1)
        yb = _sb_sample(qb.reshape(ns, 1, SB_WIDTH), ck_pages, cv_pages, l, page_table,
                        sb_logit_bias[l], norm_b_g[l])
        yc, ret_s = _ret_sample(qc[0], kc[0], vc[0], gc[0], cos_s, sin_s, state_ret, l, ret_s,
                                ret_gn_g[l], ret_gn_b[l])
        x1 = _outproj(ya.reshape(1, ns, S5_WIDTH), yb.reshape(1, ns, SB_WIDTH), yc.reshape(1, ns, RET_WIDTH),
                      xs, w_out_b, l, ln1_g[l], ln1_b[l], ns)
        x2, conv_s = _ffn_sample(x1[0], state_conv[l], w_up_b, ffn_conv_w[l], cb, w_down_b, l,
                                 ln2_g[l], ln2_b[l], 1408)
        xs = x2.reshape(1, ns, d)
        for i, a in enumerate((s5r.reshape(ns, S5_GROUPS, S5_STATE), s5i.reshape(ns, S5_GROUPS, S5_STATE), conv_s)):
            outs[4 + i].append(a)

    stk = lambda i: jnp.stack(outs[i], axis=0)
    return (xp, xs.reshape(ns, 1, d),
            token_major(kt_p), token_major(vt_p),
            token_major(kt_s).reshape(DEPTH, ns, 1, SB_HEADS, SB_HEAD_DIM),
            token_major(vt_s).reshape(DEPTH, ns, 1, SB_HEADS, SB_HEAD_DIM),
            stk(0), stk(1), stk(4), stk(5),
            stk(2), ret_s,
            stk(3), stk(6))
```
